```python
import jax, jax.numpy as jnp
from jax import lax
import numpy as np

D_MODEL = 1024
BATCH = 8
SEQ = 2048
DEPTH = 4
DEC_BATCH = 128
DEC_SEQ = 1
PAST_LEN = 16384
PAGE_SIZE = 128

N_EVEN = (DEPTH + 1) // 2
N_ODD = DEPTH // 2
CHUNK = 128
W_A = D_MODEL // 2
H_A = 8
DH_A = W_A // H_A
W_B = D_MODEL // 2
H_B = 8
DH_B = W_B // H_B
CONV_W = 4
LRU_C = 8.0
D_IN = 2 * W_A + 2 * W_B
POOL_WINDOWS = (2, 4, 8, 16)
N_POOL = len(POOL_WINDOWS)
G_POOL = D_MODEL // N_POOL
POOL_HIST = max(POOL_WINDOWS) - 1
D_FF = 4 * D_MODEL
EPS = 1e-6

kernel_name = 'hybrid_sgu_rglru_pool_decoder_step'


def rmsnorm(x, g):
    xf = x.astype(jnp.float32)
    y = xf * lax.rsqrt(jnp.mean(xf * xf, axis=-1, keepdims=True) + EPS)
    return (y * g.astype(jnp.float32)).astype(x.dtype)


def chunk_spatial_gate(u, v, ws, bs):
    b, l, _ = v.shape
    n_chunks = -(-l // CHUNK)
    pad = n_chunks * CHUNK - l
    vp = jnp.pad(v, ((0, 0), (0, pad), (0, 0))).reshape(b, n_chunks, CHUNK, H_A, DH_A)
    mask = jnp.tril(jnp.ones((CHUNK, CHUNK), dtype=bool))
    wm = jnp.where(mask, ws, 0).astype(v.dtype)
    s = jnp.einsum('hts,bcshd->bcthd', wm, vp) + bs.T.astype(v.dtype)[None, None, :, :, None]
    s = s.reshape(b, n_chunks * CHUNK, W_A)[:, :l]
    return u * s


def causal_conv(x, hist, w, bias):
    xx = jnp.concatenate([hist.astype(x.dtype), x], axis=1)
    l = x.shape[1]
    y = bias + sum(xx[:, k:k + l] * w[k] for k in range(CONV_W))
    return y, xx[:, -(CONV_W - 1):]


def rg_lru(xc, h0, wa, ba, wx, bx, lam):
    b, l, _ = xc.shape
    xh = xc.reshape(b, l, H_B, DH_B)
    r = jax.nn.sigmoid(jnp.einsum('blhi,hij->blhj', xh, wa).reshape(b, l, W_B) + ba)
    i = jax.nn.sigmoid(jnp.einsum('blhi,hij->blhj', xh, wx).reshape(b, l, W_B) + bx)
    log_a = (-LRU_C * r.astype(jnp.float32)) * jax.nn.softplus(-lam.astype(jnp.float32))
    a = jnp.exp(log_a)
    mult = jnp.sqrt(-jnp.expm1(2.0 * log_a))
    bt = mult * (i * xc).astype(jnp.float32)

    def step(h, ab):
        a_t, b_t = ab
        h = a_t * h + b_t
        return h, h

    h_last, hs = lax.scan(step, h0.astype(jnp.float32), (jnp.swapaxes(a, 0, 1), jnp.swapaxes(bt, 0, 1)))
    return jnp.swapaxes(hs, 0, 1).astype(xc.dtype), h_last.astype(h0.dtype)


def even_mixer(xn, conv_hist, h0, w_in, w_out, v_norm, sgu_w, sgu_b, conv_w, conv_b,
               gate_a_w, gate_a_b, gate_x_w, gate_x_b, lru_lambda):
    proj = xn @ w_in
    u = jax.nn.gelu(proj[..., :W_A])
    v = rmsnorm(jax.nn.gelu(proj[..., W_A:2 * W_A]), v_norm)
    gate = proj[..., 2 * W_A:2 * W_A + W_B]
    xb = proj[..., 2 * W_A + W_B:]
    a_out = chunk_spatial_gate(u, v, sgu_w, sgu_b)
    xc, conv_new = causal_conv(xb, conv_hist, conv_w, conv_b)
    hs, h_new = rg_lru(xc, h0, gate_a_w, gate_a_b, gate_x_w, gate_x_b, lru_lambda)
    b_out = hs * jax.nn.gelu(gate)
    y = jnp.concatenate([a_out, b_out], axis=-1) @ w_out
    return y, v, conv_new, h_new


def multi_pool(xn, hist, start_pos, wp, bp, scale):
    b, l, d = xn.shape
    z = jnp.concatenate([hist.astype(xn.dtype), xn], axis=1)
    zf = z.astype(jnp.float32)
    cs = jnp.concatenate([jnp.zeros((b, 1, d), jnp.float32), jnp.cumsum(zf, axis=1)], axis=1)
    pos = start_pos + jnp.arange(l)
    outs = []
    for g, w in enumerate(POOL_WINDOWS):
        sl = slice(g * G_POOL, (g + 1) * G_POOL)
        wsum = cs[:, POOL_HIST + 1:POOL_HIST + 1 + l, sl] - cs[:, POOL_HIST + 1 - w:POOL_HIST + 1 - w + l, sl]
        cnt = jnp.minimum(pos + 1, w).astype(jnp.float32)[None, :, None]
        outs.append(wsum / cnt - xn[..., sl].astype(jnp.float32))
    p = jnp.stack(outs, axis=2).astype(xn.dtype)
    y = jnp.einsum('blgi,gij->blgj', p, wp).reshape(b, l, d) + bp
    return y * scale, z[:, -POOL_HIST:]


def channel_mlp(xn, w1, w2):
    return jnp.square(jax.nn.relu(xn @ w1)) @ w2


def setup_inputs(seed: int = 0) -> dict:
    key = jax.random.key(seed)
    ks = jax.random.split(key, 32)
    f32 = jnp.float32

    def nrm(k, shape, s):
        return jax.random.normal(k, shape, f32) * s

    a0 = jax.random.uniform(ks[15], (N_EVEN, W_B), f32, 0.9, 0.999)
    p = a0 ** (1.0 / LRU_C)
    lru_lambda = jnp.log(p) - jnp.log1p(-p)
    return {
        'x_prompt': nrm(ks[0], (BATCH, SEQ, D_MODEL), 1.0),
        'x_sample': nrm(ks[1], (DEC_BATCH, DEC_SEQ, D_MODEL), 1.0),
        'state_conv': nrm(ks[2], (N_EVEN, DEC_BATCH, CONV_W - 1, W_B), 1.0),
        'state_rglru': nrm(ks[3], (N_EVEN, DEC_BATCH, W_B), 0.5),
        'state_pool': nrm(ks[4], (N_ODD, DEC_BATCH, POOL_HIST, D_MODEL), 1.0),
        'norm_mix': 1.0 + nrm(ks[5], (DEPTH, D_MODEL), 0.05),
        'norm_ffn': 1.0 + nrm(ks[6], (DEPTH, D_MODEL), 0.05),
        'norm_final': 1.0 + nrm(ks[7], (D_MODEL,), 0.05),
        'w_in': nrm(ks[8], (N_EVEN, D_MODEL, D_IN), D_MODEL ** -0.5),
        'w_out': nrm(ks[9], (N_EVEN, W_A + W_B, D_MODEL), (W_A + W_B) ** -0.5),
        'v_norm': 1.0 + nrm(ks[10], (N_EVEN, W_A), 0.05),
        'sgu_w': nrm(ks[11], (N_EVEN, H_A, CHUNK, CHUNK), CHUNK ** -0.5),
        'sgu_b': 1.0 + nrm(ks[12], (N_EVEN, H_A, CHUNK), 0.1),
        'conv_w': nrm(ks[13], (N_EVEN, CONV_W, W_B), CONV_W ** -0.5),
        'conv_b': nrm(ks[14], (N_EVEN, W_B), 0.01),
        'gate_a_w': nrm(ks[16], (N_EVEN, H_B, DH_B, DH_B), DH_B ** -0.5),
        'gate_a_b': nrm(ks[17], (N_EVEN, W_B), 0.01),
        'gate_x_w': nrm(ks[18], (N_EVEN, H_B, DH_B, DH_B), DH_B ** -0.5),
        'gate_x_b': nrm(ks[19], (N_EVEN, W_B), 0.01),
        'lru_lambda': lru_lambda,
        'pool_w': nrm(ks[20], (N_ODD, N_POOL, G_POOL, G_POOL), G_POOL ** -0.5),
        'pool_b': nrm(ks[21], (N_ODD, D_MODEL), 0.01),
        'pool_scale': 1.0 + nrm(ks[22], (N_ODD, D_MODEL), 0.1),
        'ffn_w1': nrm(ks[23], (DEPTH, D_MODEL, D_FF), D_MODEL ** -0.5),
        'ffn_w2': nrm(ks[24], (DEPTH, D_FF, D_MODEL), D_FF ** -0.5),
    }


def reference(x_prompt, x_sample, state_conv, state_rglru, state_pool,
              norm_mix, norm_ffn, norm_final, w_in, w_out, v_norm, sgu_w, sgu_b,
              conv_w, conv_b, gate_a_w, gate_a_b, gate_x_w, gate_x_b, lru_lambda,
              pool_w, pool_b, pool_scale, ffn_w1, ffn_w2):
    xp, xs = x_prompt, x_sample
    n_p = xp.shape[0]
    sgu_v_s, conv_p, conv_s, h_p, h_s, pool_p, pool_s = [], [], [], [], [], [], []
    for layer in range(DEPTH):
        g = norm_mix[layer]
        if layer % 2 == 0:
            e = layer // 2
            prm = (w_in[e], w_out[e], v_norm[e], sgu_w[e], sgu_b[e], conv_w[e], conv_b[e],
                   gate_a_w[e], gate_a_b[e], gate_x_w[e], gate_x_b[e], lru_lambda[e])
            zero_conv = jnp.zeros((n_p, CONV_W - 1, W_B), xp.dtype)
            zero_h = jnp.zeros((n_p, W_B), state_rglru.dtype)
            yp, _, cp, hp = even_mixer(rmsnorm(xp, g), zero_conv, zero_h, *prm)
            ys, vs, cs_, hs_ = even_mixer(rmsnorm(xs, g), state_conv[e], state_rglru[e], *prm)
            sgu_v_s.append(vs)
            conv_p.append(cp)
            conv_s.append(cs_)
            h_p.append(hp)
            h_s.append(hs_)
        else:
            o = layer // 2
            zero_hist = jnp.zeros((n_p, POOL_HIST, D_MODEL), xp.dtype)
            yp, pp = multi_pool(rmsnorm(xp, g), zero_hist, 0, pool_w[o], pool_b[o], pool_scale[o])
            ys, ps = multi_pool(rmsnorm(xs, g), state_pool[o], PAST_LEN, pool_w[o], pool_b[o], pool_scale[o])
            pool_p.append(pp)
            pool_s.append(ps)
        xp = xp + yp
        xs = xs + ys
        xp = xp + channel_mlp(rmsnorm(xp, norm_ffn[layer]), ffn_w1[layer], ffn_w2[layer])
        xs = xs + channel_mlp(rmsnorm(xs, norm_ffn[layer]), ffn_w1[layer], ffn_w2[layer])
    y_prompt = rmsnorm(xp, norm_final)
    y_sample = rmsnorm(xs, norm_final)
    return (y_prompt, y_sample, jnp.stack(sgu_v_s), jnp.stack(conv_p), jnp.stack(conv_s),
            jnp.stack(h_p), jnp.stack(h_s), jnp.stack(pool_p), jnp.stack(pool_s))
```

```python
import functools

import jax
import jax.numpy as jnp
from jax import lax
from jax.experimental import pallas as pl
from jax.experimental.pallas import tpu as pltpu

D_MODEL = 1024
CHUNK = 128
W_A = D_MODEL // 2
H_A = 8
DH_A = W_A // H_A
W_B = D_MODEL // 2
H_B = 8
DH_B = W_B // H_B
CONV_W = 4
LRU_C = 8.0
POOL_WINDOWS = (2, 4, 8, 16)
N_POOL = len(POOL_WINDOWS)
G_POOL = D_MODEL // N_POOL
POOL_HIST = max(POOL_WINDOWS) - 1
D_FF = 4 * D_MODEL
EPS = 1e-6
PAST_LEN = 16384

BF16 = jnp.bfloat16
F32 = jnp.float32

VMEM_LIMIT_BYTES = 56 * 1024 * 1024
LANES = 128
EVEN_NB = 4
POOL_T = 512
MLP_TM = 512
FF_CHUNK = 1024
HIST_PAD = 16


def _rms(x, g):
    return x * lax.rsqrt(jnp.mean(x * x, axis=-1, keepdims=True) + EPS) * g


def _dot(a, b):
    return jnp.dot(a, b, preferred_element_type=F32)


def _softplus(z):
    return jnp.maximum(z, 0.0) + jnp.log1p(jnp.exp(-jnp.abs(z)))


def _const_spec(shape):
    nd = len(shape)
    return pl.BlockSpec(shape, lambda *_: (0,) * nd)


def _params(n_axes):
    return pltpu.CompilerParams(
        dimension_semantics=("arbitrary",) * n_axes,
        vmem_limit_bytes=VMEM_LIMIT_BYTES)


def _lru_coeffs(xc, wg_ref, bg_ref, lam_ref):
    gates = _dot(xc.astype(BF16), wg_ref[...]) + bg_ref[...]
    r = jax.nn.sigmoid(gates[:, :W_B])
    i = jax.nn.sigmoid(gates[:, W_B:])
    log_a = (-LRU_C * r) * _softplus(-lam_ref[...])
    a = jnp.exp(log_a)
    t = jnp.tanh(log_a)
    mult = jnp.sqrt(-2.0 * t / (1.0 - t))
    return a, mult * (i * xc)


def _even_prompt_kernel(x_ref, g_ref, win_ref, wout_ref, vn_ref, sw_ref, sb_ref,
                        cw_ref, cb_ref, wg_ref, bg_ref, lam_ref,
                        xo_ref, conv_ref, h_ref,
                        xb_buf, s_buf, a_buf, b_buf, h_carry):
    nb = x_ref.shape[0]
    rows = nb * CHUNK
    j = pl.program_id(1)
    h0 = HIST_PAD - (CONV_W - 1)

    @pl.when(j == 0)
    def _():
        xb_buf[:, 0:HIST_PAD, :] = jnp.zeros((nb, HIST_PAD, W_B), F32)
        h_carry[...] = jnp.zeros_like(h_carry)

    x = x_ref[...].reshape(rows, D_MODEL)
    xn = _rms(x, g_ref[...]).astype(BF16)
    proj = _dot(xn, win_ref[...])
    u = jax.nn.gelu(proj[:, :W_A])
    v = _rms(jax.nn.gelu(proj[:, W_A:2 * W_A]), vn_ref[...])
    gate = proj[:, 2 * W_A:2 * W_A + W_B]
    xb = proj[:, 2 * W_A + W_B:]

    vb = v.astype(BF16)
    lane = lax.broadcasted_iota(jnp.int32, (rows, 2 * DH_A), 1)
    for p in range(H_A // 2):
        vp = vb[:, p * 2 * DH_A:(p + 1) * 2 * DH_A]
        lo = jnp.where(lane < DH_A, vp, jnp.zeros_like(vp))
        hi = jnp.where(lane >= DH_A, vp, jnp.zeros_like(vp))
        rhs = jnp.concatenate(
            [jnp.concatenate([lo[b * CHUNK:(b + 1) * CHUNK] for b in range(nb)], axis=1),
             jnp.concatenate([hi[b * CHUNK:(b + 1) * CHUNK] for b in range(nb)], axis=1)],
            axis=0)
        o = _dot(sw_ref[p], rhs)
        for b in range(nb):
            s_buf[b * CHUNK:(b + 1) * CHUNK, p * 2 * DH_A:(p + 1) * 2 * DH_A] = (
                o[:, b * CHUNK:(b + 1) * CHUNK])
    sb = sb_ref[...]
    s = (s_buf[...].reshape(nb, CHUNK, W_A) + sb[None]).reshape(rows, W_A)
    a_out = (u * s).astype(BF16)

    xb_buf[:, HIST_PAD:HIST_PAD + CHUNK, :] = xb.reshape(nb, CHUNK, W_B)
    cw = cw_ref[...]
    xc = cb_ref[...].reshape(1, 1, W_B) + sum(
        xb_buf[:, h0 + k:h0 + k + CHUNK, :] * cw[k].reshape(1, 1, W_B)
        for k in range(CONV_W))
    xc = xc.reshape(rows, W_B)
    tail = xb_buf[:, HIST_PAD + CHUNK - (CONV_W - 1):HIST_PAD + CHUNK, :]
    xb_buf[:, h0:HIST_PAD, :] = tail

    a, bt = _lru_coeffs(xc, wg_ref, bg_ref, lam_ref)
    n_col = W_B // LANES
    for c in range(n_col):
        a_buf[c] = a[:, c * LANES:(c + 1) * LANES]
        b_buf[c] = bt[:, c * LANES:(c + 1) * LANES]

    def step(t, hs):
        sl = pl.ds(t, nb, stride=CHUNK)
        new = []
        for c in range(n_col):
            h = a_buf[c, sl, :] * hs[c] + b_buf[c, sl, :]
            b_buf[c, sl, :] = h
            new.append(h)
        return tuple(new)

    h_init = tuple(h_carry[:, c * LANES:(c + 1) * LANES] for c in range(n_col))
    h_cols = lax.fori_loop(0, CHUNK, step, h_init, unroll=8)
    h_last = jnp.concatenate(h_cols, axis=1)
    h_carry[...] = h_last

    hs_all = jnp.concatenate([b_buf[c] for c in range(n_col)], axis=1)
    b_out = (hs_all * jax.nn.gelu(gate)).astype(BF16)
    y = _dot(a_out, wout_ref[0:W_A, :]) + _dot(b_out, wout_ref[W_A:, :])
    xo_ref[...] = (x + y).reshape(nb, CHUNK, D_MODEL)

    @pl.when(j == pl.num_programs(1) - 1)
    def _():
        conv_ref[...] = tail
        h_ref[0] = h_last


def _even_prompt(x, g, win, wout, vn, sw, sb, cw, cb, wg, bg, lam):
    n_b, seq, _ = x.shape
    nb = EVEN_NB
    grid = (n_b // nb, seq // CHUNK)
    x_spec = pl.BlockSpec((nb, CHUNK, D_MODEL), lambda i, j: (i, j, 0))
    consts = (g, win, wout, vn, sw, sb, cw, cb, wg, bg, lam)
    return pl.pallas_call(
        _even_prompt_kernel,
        grid=grid,
        in_specs=[x_spec] + [_const_spec(c.shape) for c in consts],
        out_specs=[x_spec,
                   pl.BlockSpec((nb, CONV_W - 1, W_B), lambda i, j: (i, 0, 0)),
                   pl.BlockSpec((1, nb, W_B), lambda i, j: (i, 0, 0))],
        out_shape=[jax.ShapeDtypeStruct(x.shape, F32),
                   jax.ShapeDtypeStruct((n_b, CONV_W - 1, W_B), F32),
                   jax.ShapeDtypeStruct((n_b // nb, nb, W_B), F32)],
        scratch_shapes=[pltpu.VMEM((nb, HIST_PAD + CHUNK, W_B), F32),
                        pltpu.VMEM((nb * CHUNK, W_A), F32),
                        pltpu.VMEM((W_B // LANES, nb * CHUNK, LANES), F32),
                        pltpu.VMEM((W_B // LANES, nb * CHUNK, LANES), F32),
                        pltpu.VMEM((nb, W_B), F32)],
        compiler_params=_params(2),
        name="even_prompt",
    )(x, *consts)


def _even_sample_kernel(x_ref, conv_ref, h0_ref, g_ref, win_ref, wout_ref, vn_ref,
                        w00_ref, b0_ref, cw_ref, cb_ref, wg_ref, bg_ref, lam_ref,
                        xo_ref, v_ref, convo_ref, ho_ref):
    x = x_ref[...]
    xn = _rms(x, g_ref[...]).astype(BF16)
    proj = _dot(xn, win_ref[...])
    u = jax.nn.gelu(proj[:, :W_A])
    v = _rms(jax.nn.gelu(proj[:, W_A:2 * W_A]), vn_ref[...])
    gate = proj[:, 2 * W_A:2 * W_A + W_B]
    xb = proj[:, 2 * W_A + W_B:]
    v_ref[...] = v
    a_out = (u * (v * w00_ref[...] + b0_ref[...])).astype(BF16)

    cw = cw_ref[...]
    xc = cb_ref[...] + xb * cw[CONV_W - 1:CONV_W, :]
    for k in range(CONV_W - 1):
        xc = xc + conv_ref[:, k * W_B:(k + 1) * W_B] * cw[k:k + 1, :]
    convo_ref[:, 0:(CONV_W - 2) * W_B] = conv_ref[:, W_B:(CONV_W - 1) * W_B]
    convo_ref[:, (CONV_W - 2) * W_B:] = xb

    a, bt = _lru_coeffs(xc, wg_ref, bg_ref, lam_ref)
    h = a * h0_ref[...] + bt
    ho_ref[...] = h
    b_out = (h * jax.nn.gelu(gate)).astype(BF16)
    y = _dot(a_out, wout_ref[0:W_A, :]) + _dot(b_out, wout_ref[W_A:, :])
    xo_ref[...] = x + y


def _even_sample(x, conv, h0, g, win, wout, vn, w00, b0, cw, cb, wg, bg, lam):
    n = x.shape[0]
    args = (x, conv, h0, g, win, wout, vn, w00, b0, cw, cb, wg, bg, lam)
    return pl.pallas_call(
        _even_sample_kernel,
        grid=(1,),
        in_specs=[_const_spec(a.shape) for a in args],
        out_specs=[_const_spec((n, D_MODEL)), _const_spec((n, W_A)),
                   _const_spec(conv.shape), _const_spec((n, W_B))],
        out_shape=[jax.ShapeDtypeStruct((n, D_MODEL), F32),
                   jax.ShapeDtypeStruct((n, W_A), F32),
                   jax.ShapeDtypeStruct(conv.shape, F32),
                   jax.ShapeDtypeStruct((n, W_B), F32)],
        compiler_params=_params(1),
        name="even_sample",
    )(*args)


def _pool_prompt_kernel(x_ref, g_ref, wp_ref, bp_ref, sc_ref, xo_ref, hist_ref, z_buf):
    t_rows = x_ref.shape[1]
    j = pl.program_id(1)

    @pl.when(j == 0)
    def _():
        z_buf[0:HIST_PAD, :] = jnp.zeros((HIST_PAD, D_MODEL), F32)

    x = x_ref[0]
    xn = _rms(x, g_ref[...])
    z_buf[HIST_PAD:HIST_PAD + t_rows, :] = xn
    pos = j * t_rows + lax.broadcasted_iota(jnp.int32, (t_rows, 1), 0)
    ys = []
    for gi, w in enumerate(POOL_WINDOWS):
        c0 = gi * G_POOL
        wsum = xn[:, c0:c0 + G_POOL]
        for d in range(1, w):
            wsum = wsum + z_buf[HIST_PAD - d:HIST_PAD - d + t_rows, c0:c0 + G_POOL]
        cnt = jnp.minimum(pos + 1, w).astype(F32)
        p = (wsum / cnt - xn[:, c0:c0 + G_POOL]).astype(BF16)
        ys.append(_dot(p, wp_ref[gi]))
    y = (jnp.concatenate(ys, axis=1) + bp_ref[...]) * sc_ref[...]
    xo_ref[0] = x + y
    tail = z_buf[HIST_PAD + t_rows - POOL_HIST:HIST_PAD + t_rows, :]
    z_buf[HIST_PAD - POOL_HIST:HIST_PAD, :] = tail

    @pl.when(j == pl.num_programs(1) - 1)
    def _():
        hist_ref[0] = tail


def _pool_prompt(x, g, wp, bp, sc):
    n_b, seq, _ = x.shape
    t_rows = POOL_T
    x_spec = pl.BlockSpec((1, t_rows, D_MODEL), lambda i, j: (i, j, 0))
    consts = (g, wp, bp, sc)
    return pl.pallas_call(
        _pool_prompt_kernel,
        grid=(n_b, seq // t_rows),
        in_specs=[x_spec] + [_const_spec(c.shape) for c in consts],
        out_specs=[x_spec, pl.BlockSpec((1, POOL_HIST, D_MODEL), lambda i, j: (i, 0, 0))],
        out_shape=[jax.ShapeDtypeStruct(x.shape, F32),
                   jax.ShapeDtypeStruct((n_b, POOL_HIST, D_MODEL), F32)],
        scratch_shapes=[pltpu.VMEM((HIST_PAD + t_rows, D_MODEL), F32)],
        compiler_params=_params(2),
        name="pool_prompt",
    )(x, *consts)


def _pool_sample_kernel(x_ref, hist_ref, g_ref, wp_ref, bp_ref, sc_ref, xo_ref, histo_ref):
    x = x_ref[...]
    xn = _rms(x, g_ref[...])
    ys = []
    for gi, w in enumerate(POOL_WINDOWS):
        c0 = gi * G_POOL
        xg = xn[:, c0:c0 + G_POOL]
        wsum = xg
        for d in range(1, w):
            off = (POOL_HIST - d) * D_MODEL + c0
            wsum = wsum + hist_ref[:, off:off + G_POOL]
        cnt = float(min(PAST_LEN + 1, w))
        p = (wsum / cnt - xg).astype(BF16)
        ys.append(_dot(p, wp_ref[gi]))
    y = (jnp.concatenate(ys, axis=1) + bp_ref[...]) * sc_ref[...]
    xo_ref[...] = x + y
    histo_ref[:, 0:(POOL_HIST - 1) * D_MODEL] = hist_ref[:, D_MODEL:]
    histo_ref[:, (POOL_HIST - 1) * D_MODEL:] = xn


def _pool_sample(x, hist, g, wp, bp, sc):
    n = x.shape[0]
    args = (x, hist, g, wp, bp, sc)
    return pl.pallas_call(
        _pool_sample_kernel,
        grid=(1,),
        in_specs=[_const_spec(a.shape) for a in args],
        out_specs=[_const_spec((n, D_MODEL)), _const_spec(hist.shape)],
        out_shape=[jax.ShapeDtypeStruct((n, D_MODEL), F32),
                   jax.ShapeDtypeStruct(hist.shape, F32)],
        compiler_params=_params(1),
        name="pool_sample",
    )(*args)


def _mlp_kernel(x_ref, g_ref, w1_ref, w2_ref, gf_ref, o_ref, *, final_norm):
    x = x_ref[...]
    xn = _rms(x, g_ref[...]).astype(BF16)
    acc = x
    for c in range(D_FF // FF_CHUNK):
        h = _dot(xn, w1_ref[:, c * FF_CHUNK:(c + 1) * FF_CHUNK])
        h = jnp.square(jnp.maximum(h, 0.0)).astype(BF16)
        acc = acc + _dot(h, w2_ref[c * FF_CHUNK:(c + 1) * FF_CHUNK, :])
    if final_norm:
        acc = _rms(acc, gf_ref[...])
    o_ref[...] = acc


def _mlp(x, g, w1, w2, gf, *, tm, final_norm):
    n = x.shape[0]
    x_spec = pl.BlockSpec((tm, D_MODEL), lambda i: (i, 0))
    consts = (g, w1, w2, gf)
    return pl.pallas_call(
        functools.partial(_mlp_kernel, final_norm=final_norm),
        grid=(n // tm,),
        in_specs=[x_spec] + [_const_spec(c.shape) for c in consts],
        out_specs=x_spec,
        out_shape=jax.ShapeDtypeStruct(x.shape, F32),
        compiler_params=_params(1),
        name="mlp",
    )(x, *consts)


def _block_diag(w):
    h, d, _ = w.shape
    eye = jnp.eye(h, dtype=w.dtype)
    return jnp.einsum('hij,hk->hikj', w, eye).reshape(h * d, h * d)


def _row(v):
    return v.reshape(1, -1)


def kernel(x_prompt, x_sample, state_conv, state_rglru, state_pool, norm_mix, norm_ffn, norm_final, w_in, w_out, v_norm, sgu_w, sgu_b, conv_w, conv_b, gate_a_w, gate_a_b, gate_x_w, gate_x_b, lru_lambda, pool_w, pool_b, pool_scale, ffn_w1, ffn_w2):
    n_p, seq, _ = x_prompt.shape
    n_s = x_sample.shape[0]
    depth = norm_mix.shape[0]
    xp = x_prompt
    xs = x_sample.reshape(n_s, D_MODEL)
    tril = jnp.tril(jnp.ones((CHUNK, CHUNK), dtype=bool))

    sgu_v_s, conv_p, conv_s, h_p, h_s, pool_p, pool_s = [], [], [], [], [], [], []
    for layer in range(depth):
        g = _row(norm_mix[layer])
        if layer % 2 == 0:
            e = layer // 2
            win = w_in[e].astype(BF16)
            wout = w_out[e].astype(BF16)
            wm = jnp.where(tril, sgu_w[e], 0.0)
            sw = jnp.concatenate([wm[0::2], wm[1::2]], axis=2).astype(BF16)
            sb = jnp.repeat(sgu_b[e].T, DH_A, axis=1)
            wg = jnp.concatenate([_block_diag(gate_a_w[e]), _block_diag(gate_x_w[e])],
                                 axis=1).astype(BF16)
            bg = _row(jnp.concatenate([gate_a_b[e], gate_x_b[e]]))
            common = (_row(v_norm[e]),)
            tail = (conv_w[e], _row(conv_b[e]), wg, bg, _row(lru_lambda[e]))
            xp, cp, hp = _even_prompt(xp, g, win, wout, *common, sw, sb, *tail)
            w00 = _row(jnp.repeat(wm[:, 0, 0], DH_A))
            b0 = _row(jnp.repeat(sgu_b[e][:, 0], DH_A))
            xs, vs, cs, hs = _even_sample(
                xs, state_conv[e].reshape(n_s, (CONV_W - 1) * W_B), state_rglru[e],
                g, win, wout, *common, w00, b0, *tail)
            sgu_v_s.append(vs.reshape(n_s, 1, W_A))
            conv_p.append(cp)
            conv_s.append(cs.reshape(n_s, CONV_W - 1, W_B))
            h_p.append(hp.reshape(n_p, W_B))
            h_s.append(hs)
        else:
            o = layer // 2
            wp = pool_w[o].astype(BF16)
            bp, sc = _row(pool_b[o]), _row(pool_scale[o])
            xp, pp = _pool_prompt(xp, g, wp, bp, sc)
            xs, ps = _pool_sample(xs, state_pool[o].reshape(n_s, POOL_HIST * D_MODEL), g, wp, bp, sc)
            pool_p.append(pp)
            pool_s.append(ps.reshape(n_s, POOL_HIST, D_MODEL))
        gf = _row(norm_ffn[layer])
        w1 = ffn_w1[layer].astype(BF16)
        w2 = ffn_w2[layer].astype(BF16)
        last = layer == depth - 1
        gfin = _row(norm_final)
        xp = _mlp(xp.reshape(n_p * seq, D_MODEL), gf, w1, w2, gfin,
                  tm=MLP_TM, final_norm=last).reshape(n_p, seq, D_MODEL)
        xs = _mlp(xs, gf, w1, w2, gfin, tm=n_s, final_norm=last)
    return (xp, xs.reshape(n_s, 1, D_MODEL), jnp.stack(sgu_v_s), jnp.stack(conv_p),
            jnp.stack(conv_s), jnp.stack(h_p), jnp.stack(h_s), jnp.stack(pool_p),
            jnp.stack(pool_s))
```

```python
import functools

import jax
import jax.numpy as jnp
from jax import lax
from jax.experimental import pallas as pl
from jax.experimental.pallas import tpu as pltpu

D_MODEL = 1024
CHUNK = 128
W_A = D_MODEL // 2
H_A = 8
DH_A = W_A // H_A
W_B = D_MODEL // 2
H_B = 8
DH_B = W_B // H_B
CONV_W = 4
LRU_C = 8.0
POOL_WINDOWS = (2, 4, 8, 16)
N_POOL = len(POOL_WINDOWS)
G_POOL = D_MODEL // N_POOL
POOL_HIST = max(POOL_WINDOWS) - 1
D_FF = 4 * D_MODEL
EPS = 1e-6
PAST_LEN = 16384

BF16 = jnp.bfloat16
F32 = jnp.float32

VMEM_LIMIT_BYTES = 56 * 1024 * 1024
LANES = 128
NB = 8
MXU_K = 256
POOL_T = 64
MLP_TM = 512
FF_CHUNK = 1024


def _rms(x, g):
    return x * lax.rsqrt(jnp.mean(x * x, axis=-1, keepdims=True) + EPS) * g


def _dot(a, b):
    return jnp.dot(a, b, preferred_element_type=F32)


def _softplus(z):
    return jnp.maximum(z, 0.0) + jnp.log1p(jnp.exp(-jnp.abs(z)))


def _const_spec(shape):
    nd = len(shape)
    return pl.BlockSpec(shape, lambda *_: (0,) * nd)


def _layer_spec(arr, layer):
    nd = arr.ndim - 1
    return pl.BlockSpec((None,) + arr.shape[1:], lambda *_: (layer,) + (0,) * nd)


def _params(n_axes):
    return pltpu.CompilerParams(
        dimension_semantics=("arbitrary",) * n_axes,
        vmem_limit_bytes=VMEM_LIMIT_BYTES)


def _lru_coeffs(xc, wg_ref, bg_ref, lam_ref):
    xcb = xc.astype(BF16)
    rs, is_ = [], []
    for kb in range(W_B // MXU_K):
        gk = _dot(xcb[:, kb * MXU_K:(kb + 1) * MXU_K], wg_ref[kb])
        rs.append(gk[:, :MXU_K])
        is_.append(gk[:, MXU_K:])
    bg = bg_ref[...]
    r = jax.nn.sigmoid(jnp.concatenate(rs, axis=1) + bg[:, :W_B])
    i = jax.nn.sigmoid(jnp.concatenate(is_, axis=1) + bg[:, W_B:])
    log_a = (-LRU_C * r) * _softplus(-lam_ref[...])
    a = jnp.exp(log_a)
    t = jnp.tanh(log_a)
    mult = jnp.sqrt(-2.0 * t / (1.0 - t))
    return a, mult * (i * xc)


def _even_prompt_kernel(x_ref, g_ref, win_ref, wout_ref, vn_ref, sw_ref, sb_ref,
                        cw_ref, cb_ref, wg_ref, bg_ref, lam_ref,
                        xo_ref, conv_ref, h_ref,
                        xb_buf, v_buf, s_buf, a_buf, b_buf, h_carry, *maybe_xt_buf,
                        batch_major_in):
    rows = NB * CHUNK
    hist_rows = (CONV_W - 1) * NB
    j = pl.program_id(0)

    @pl.when(j == 0)
    def _():
        xb_buf[0:hist_rows, :] = jnp.zeros((hist_rows, W_B), F32)
        h_carry[...] = jnp.zeros_like(h_carry)

    if batch_major_in:
        xt_buf, = maybe_xt_buf
        for b in range(NB):
            for c in range(D_MODEL // LANES):
                xt_buf[c, pl.ds(b, CHUNK, stride=NB), :] = x_ref[b, :, c * LANES:(c + 1) * LANES]
        x = jnp.concatenate([xt_buf[c] for c in range(D_MODEL // LANES)], axis=1)
    else:
        x = x_ref[...]
    xn = _rms(x, g_ref[...]).astype(BF16)
    proj_a = _dot(xn, win_ref[:, 0:2 * W_A])
    u = jax.nn.gelu(proj_a[:, :W_A])
    v = _rms(jax.nn.gelu(proj_a[:, W_A:]), vn_ref[...])
    for p in range(W_A // LANES):
        v_buf[p] = v[:, p * LANES:(p + 1) * LANES]

    lane = lax.broadcasted_iota(jnp.int32, (CHUNK, LANES), 1)
    zero = jnp.zeros((CHUNK, LANES), BF16)
    for p in range(W_A // LANES):
        los, his = [], []
        for b in range(NB):
            vb = v_buf[p, pl.ds(b, CHUNK, stride=NB), :].astype(BF16)
            los.append(jnp.where(lane < DH_A, vb, zero))
            his.append(jnp.where(lane >= DH_A, vb, zero))
        rhs = jnp.concatenate([jnp.concatenate(los, axis=1),
                               jnp.concatenate(his, axis=1)], axis=0)
        o = _dot(sw_ref[p], rhs)
        sbp = sb_ref[:, p * LANES:(p + 1) * LANES]
        for b in range(NB):
            s_buf[p, pl.ds(b, CHUNK, stride=NB), :] = o[:, b * LANES:(b + 1) * LANES] + sbp
    s = jnp.concatenate([s_buf[p] for p in range(W_A // LANES)], axis=1)
    a_out = (u * s).astype(BF16)

    proj_b = _dot(xn, win_ref[:, 2 * W_A:])
    gate_act = jax.nn.gelu(proj_b[:, :W_B])
    xb_buf[hist_rows:hist_rows + rows, :] = proj_b[:, W_B:]

    cw = cw_ref[...]
    xc = cb_ref[...] + sum(xb_buf[k * NB:k * NB + rows, :] * cw[k:k + 1, :]
                           for k in range(CONV_W))
    tail = xb_buf[rows:rows + hist_rows, :]
    xb_buf[0:hist_rows, :] = tail

    a, bt = _lru_coeffs(xc, wg_ref, bg_ref, lam_ref)
    a_buf[...] = a
    b_buf[...] = bt

    def step(t, h):
        sl = pl.ds(pl.multiple_of(t * NB, NB), NB)
        h = a_buf[sl, :] * h + b_buf[sl, :]
        b_buf[sl, :] = h
        return h

    h_last = lax.fori_loop(0, CHUNK, step, h_carry[...], unroll=8)
    h_carry[...] = h_last

    b_out = (b_buf[...] * gate_act).astype(BF16)
    y = _dot(a_out, wout_ref[0:W_A, :]) + _dot(b_out, wout_ref[W_A:, :])
    xo_ref[...] = x + y

    @pl.when(j == pl.num_programs(0) - 1)
    def _():
        conv_ref[...] = tail.reshape(CONV_W - 1, NB, W_B)
        h_ref[...] = h_last


def _even_prompt(x, layer_params, e, *, batch_major_in):
    seq = x.shape[1] if batch_major_in else x.shape[0] // NB
    rows = NB * CHUNK
    if batch_major_in:
        x_spec = pl.BlockSpec((NB, CHUNK, D_MODEL), lambda j: (0, j, 0))
    else:
        x_spec = pl.BlockSpec((rows, D_MODEL), lambda j: (j, 0))
    slab = lambda width: pltpu.VMEM((width // LANES, rows, LANES), F32)
    scratch = [pltpu.VMEM(((CONV_W - 1) * NB + rows, W_B), F32),
               slab(W_A), slab(W_A),
               pltpu.VMEM((rows, W_B), F32), pltpu.VMEM((rows, W_B), F32),
               pltpu.VMEM((NB, W_B), F32)]
    if batch_major_in:
        scratch.append(slab(D_MODEL))
    return pl.pallas_call(
        functools.partial(_even_prompt_kernel, batch_major_in=batch_major_in),
        grid=(seq // CHUNK,),
        in_specs=[x_spec] + [_layer_spec(p, e) for p in layer_params],
        out_specs=[pl.BlockSpec((rows, D_MODEL), lambda j: (j, 0)),
                   _const_spec((CONV_W - 1, NB, W_B)),
                   _const_spec((NB, W_B))],
        out_shape=[jax.ShapeDtypeStruct((seq * NB, D_MODEL), F32),
                   jax.ShapeDtypeStruct((CONV_W - 1, NB, W_B), F32),
                   jax.ShapeDtypeStruct((NB, W_B), F32)],
        scratch_shapes=scratch,
        compiler_params=_params(1),
        name="even_prompt",
    )(x, *layer_params)


def _even_sample_kernel(x_ref, conv_ref, h0_ref, g_ref, win_ref, wout_ref, vn_ref,
                        w00_ref, b0_ref, cw_ref, cb_ref, wg_ref, bg_ref, lam_ref,
                        xo_ref, v_ref, convo_ref, ho_ref):
    x = x_ref[...]
    xn = _rms(x, g_ref[...]).astype(BF16)
    proj = _dot(xn, win_ref[...])
    u = jax.nn.gelu(proj[:, :W_A])
    v = _rms(jax.nn.gelu(proj[:, W_A:2 * W_A]), vn_ref[...])
    gate = proj[:, 2 * W_A:2 * W_A + W_B]
    xb = proj[:, 2 * W_A + W_B:]
    v_ref[...] = v
    a_out = (u * (v * w00_ref[...] + b0_ref[...])).astype(BF16)

    cw = cw_ref[...]
    xc = cb_ref[...] + xb * cw[CONV_W - 1:CONV_W, :]
    for k in range(CONV_W - 1):
        hk = conv_ref[:, k, :]
        xc = xc + hk * cw[k:k + 1, :]
        if k > 0:
            convo_ref[:, k - 1, :] = hk
    convo_ref[:, CONV_W - 2, :] = xb

    a, bt = _lru_coeffs(xc, wg_ref, bg_ref, lam_ref)
    h = a * h0_ref[...] + bt
    ho_ref[...] = h
    b_out = (h * jax.nn.gelu(gate)).astype(BF16)
    y = _dot(a_out, wout_ref[0:W_A, :]) + _dot(b_out, wout_ref[W_A:, :])
    xo_ref[...] = x + y


def _even_sample(x, state_conv, state_h, layer_params, e):
    n = x.shape[0]
    return pl.pallas_call(
        _even_sample_kernel,
        grid=(1,),
        in_specs=[_const_spec(x.shape), _layer_spec(state_conv, e), _layer_spec(state_h, e)]
                 + [_layer_spec(p, e) for p in layer_params],
        out_specs=[_const_spec((n, D_MODEL)), _const_spec((n, W_A)),
                   _const_spec(state_conv.shape[1:]), _const_spec((n, W_B))],
        out_shape=[jax.ShapeDtypeStruct((n, D_MODEL), F32),
                   jax.ShapeDtypeStruct((n, W_A), F32),
                   jax.ShapeDtypeStruct(state_conv.shape[1:], F32),
                   jax.ShapeDtypeStruct((n, W_B), F32)],
        compiler_params=_params(1),
        name="even_sample",
    )(x, state_conv, state_h, *layer_params)


def _pool_prompt_kernel(x_ref, g_ref, wp_ref, bp_ref, sc_ref, xo_ref, hist_ref, z_buf):
    rows = x_ref.shape[0]
    t_steps = rows // NB
    pad = (POOL_HIST + 1) * NB
    j = pl.program_id(0)

    @pl.when(j == 0)
    def _():
        z_buf[0:pad, :] = jnp.zeros((pad, D_MODEL), F32)

    x = x_ref[...]
    xn = _rms(x, g_ref[...])
    z_buf[pad:pad + rows, :] = xn
    row = lax.broadcasted_iota(jnp.int32, (rows, 1), 0)
    pos = j * t_steps + jnp.right_shift(row, NB.bit_length() - 1)
    ys = []
    for gi, w in enumerate(POOL_WINDOWS):
        c0 = gi * G_POOL
        acc = z_buf[pad - (w - 1) * NB:pad + rows, c0:c0 + G_POOL]
        span = 1
        while span < w:
            acc = acc[span * NB:, :] + acc[:-span * NB, :]
            span *= 2
        cnt = jnp.minimum(pos + 1, w).astype(F32)
        p = (acc / cnt - xn[:, c0:c0 + G_POOL]).astype(BF16)
        ys.append(_dot(p, wp_ref[gi]))
    y = (jnp.concatenate(ys, axis=1) + bp_ref[...]) * sc_ref[...]
    xo_ref[...] = x + y
    tail = z_buf[rows + NB:rows + pad, :]
    z_buf[NB:pad, :] = tail

    @pl.when(j == pl.num_programs(0) - 1)
    def _():
        hist_ref[...] = tail.reshape(POOL_HIST, NB, D_MODEL)


def _pool_prompt(x, layer_params, o):
    n_rows = x.shape[0]
    rows = POOL_T * NB
    x_spec = pl.BlockSpec((rows, D_MODEL), lambda j: (j, 0))
    return pl.pallas_call(
        _pool_prompt_kernel,
        grid=(n_rows // rows,),
        in_specs=[x_spec] + [_layer_spec(p, o) for p in layer_params],
        out_specs=[x_spec, _const_spec((POOL_HIST, NB, D_MODEL))],
        out_shape=[jax.ShapeDtypeStruct(x.shape, F32),
                   jax.ShapeDtypeStruct((POOL_HIST, NB, D_MODEL), F32)],
        scratch_shapes=[pltpu.VMEM(((POOL_HIST + 1) * NB + rows, D_MODEL), F32)],
        compiler_params=_params(1),
        name="pool_prompt",
    )(x, *layer_params)


def _pool_sample_kernel(x_ref, hist_ref, g_ref, wp_ref, bp_ref, sc_ref, xo_ref, histo_ref):
    x = x_ref[...]
    xn = _rms(x, g_ref[...])
    ys = []
    for gi, w in enumerate(POOL_WINDOWS):
        c0 = gi * G_POOL
        xg = xn[:, c0:c0 + G_POOL]
        wsum = xg
        for d in range(1, w):
            wsum = wsum + hist_ref[:, POOL_HIST - d, c0:c0 + G_POOL]
        cnt = float(min(PAST_LEN + 1, w))
        p = (wsum / cnt - xg).astype(BF16)
        ys.append(_dot(p, wp_ref[gi]))
    y = (jnp.concatenate(ys, axis=1) + bp_ref[...]) * sc_ref[...]
    xo_ref[...] = x + y
    for k in range(1, POOL_HIST):
        histo_ref[:, k - 1, :] = hist_ref[:, k, :]
    histo_ref[:, POOL_HIST - 1, :] = xn


def _pool_sample(x, state_pool, layer_params, o):
    n = x.shape[0]
    return pl.pallas_call(
        _pool_sample_kernel,
        grid=(1,),
        in_specs=[_const_spec(x.shape), _layer_spec(state_pool, o)]
                 + [_layer_spec(p, o) for p in layer_params],
        out_specs=[_const_spec((n, D_MODEL)), _const_spec(state_pool.shape[1:])],
        out_shape=[jax.ShapeDtypeStruct((n, D_MODEL), F32),
                   jax.ShapeDtypeStruct(state_pool.shape[1:], F32)],
        compiler_params=_params(1),
        name="pool_sample",
    )(x, state_pool, *layer_params)


def _mlp_kernel(x_ref, g_ref, w1_ref, w2_ref, gf_ref, o_ref, *maybe_ot_buf,
                final_norm, batch_major_out):
    x = x_ref[...]
    xn = _rms(x, g_ref[...]).astype(BF16)
    acc = x
    for c in range(D_FF // FF_CHUNK):
        h = _dot(xn, w1_ref[:, c * FF_CHUNK:(c + 1) * FF_CHUNK])
        h = jnp.square(jnp.maximum(h, 0.0)).astype(BF16)
        acc = acc + _dot(h, w2_ref[c * FF_CHUNK:(c + 1) * FF_CHUNK, :])
    if final_norm:
        acc = _rms(acc, gf_ref[...])
    if batch_major_out:
        ot_buf, = maybe_ot_buf
        t_steps = x.shape[0] // NB
        for c in range(D_MODEL // LANES):
            ot_buf[c] = acc[:, c * LANES:(c + 1) * LANES]
        for b in range(NB):
            for c in range(D_MODEL // LANES):
                o_ref[b, :, c * LANES:(c + 1) * LANES] = ot_buf[c, pl.ds(b, t_steps, stride=NB), :]
    else:
        o_ref[...] = acc


def _mlp(x, g, w1, w2, gf, layer, *, tm, final_norm, batch_major_out=False):
    n = x.shape[0]
    x_spec = pl.BlockSpec((tm, D_MODEL), lambda i: (i, 0))
    if batch_major_out:
        t_steps = tm // NB
        out_spec = pl.BlockSpec((NB, t_steps, D_MODEL), lambda i: (0, i, 0))
        out_shape = jax.ShapeDtypeStruct((NB, n // NB, D_MODEL), F32)
        scratch = [pltpu.VMEM((D_MODEL // LANES, tm, LANES), F32)]
    else:
        out_spec, out_shape, scratch = x_spec, jax.ShapeDtypeStruct(x.shape, F32), []
    return pl.pallas_call(
        functools.partial(_mlp_kernel, final_norm=final_norm, batch_major_out=batch_major_out),
        grid=(n // tm,),
        in_specs=[x_spec, _layer_spec(g, layer), _layer_spec(w1, layer), _layer_spec(w2, layer),
                  _const_spec(gf.shape)],
        out_specs=out_spec,
        out_shape=out_shape,
        scratch_shapes=scratch,
        compiler_params=_params(1),
        name="mlp",
    )(x, g, w1, w2, gf)


def _gate_blocks(wa, wx):
    n_e = wa.shape[0]
    hpb = MXU_K // DH_B
    eye = jnp.eye(hpb, dtype=wa.dtype)

    def bd(w):
        w = w.reshape(n_e, H_B // hpb, hpb, DH_B, DH_B)
        return jnp.einsum('eghij,hk->eghikj', w, eye).reshape(n_e, H_B // hpb, MXU_K, MXU_K)

    return jnp.concatenate([bd(wa), bd(wx)], axis=-1)


def _rows(v):
    return v.reshape(v.shape[0], 1, v.shape[1])


def kernel(x_prompt, x_sample, state_conv, state_rglru, state_pool, norm_mix, norm_ffn, norm_final, w_in, w_out, v_norm, sgu_w, sgu_b, conv_w, conv_b, gate_a_w, gate_a_b, gate_x_w, gate_x_b, lru_lambda, pool_w, pool_b, pool_scale, ffn_w1, ffn_w2):
    n_p, seq, _ = x_prompt.shape
    n_s = x_sample.shape[0]
    depth = norm_mix.shape[0]
    assert n_p == NB and seq % CHUNK == 0

    tril = jnp.tril(jnp.ones((CHUNK, CHUNK), dtype=bool))
    wm = jnp.where(tril, sgu_w, 0.0)
    sw = jnp.concatenate([wm[:, 0::2], wm[:, 1::2]], axis=3).astype(BF16)
    sb = jnp.repeat(jnp.swapaxes(sgu_b, 1, 2), DH_A, axis=2)
    wg = _gate_blocks(gate_a_w, gate_x_w).astype(BF16)
    bg = _rows(jnp.concatenate([gate_a_b, gate_x_b], axis=1))
    w00 = _rows(jnp.repeat(wm[:, :, 0, 0], DH_A, axis=1))
    b0 = _rows(jnp.repeat(sgu_b[:, :, 0], DH_A, axis=1))
    win, wout = w_in.astype(BF16), w_out.astype(BF16)
    w1, w2 = ffn_w1.astype(BF16), ffn_w2.astype(BF16)
    wp = pool_w.astype(BF16)
    g_mix_even, g_mix_odd = _rows(norm_mix[0::2]), _rows(norm_mix[1::2])
    g_ffn = _rows(norm_ffn)
    g_fin = norm_final.reshape(1, D_MODEL)
    even_tail = (conv_w, _rows(conv_b), wg, bg, _rows(lru_lambda))
    even_p = (g_mix_even, win, wout, _rows(v_norm), sw, sb) + even_tail
    even_s = (g_mix_even, win, wout, _rows(v_norm), w00, b0) + even_tail
    pool_params = (g_mix_odd, wp, _rows(pool_b), _rows(pool_scale))

    xp = x_prompt
    xs = x_sample.reshape(n_s, D_MODEL)
    sgu_v_s, conv_p, conv_s, h_p, h_s, pool_p, pool_s = [], [], [], [], [], [], []
    for layer in range(depth):
        if layer % 2 == 0:
            e = layer // 2
            xp, cp, hp = _even_prompt(xp, even_p, e, batch_major_in=(layer == 0))
            xs, vs, cs, hs = _even_sample(xs, state_conv, state_rglru, even_s, e)
            sgu_v_s.append(vs.reshape(n_s, 1, W_A))
            conv_p.append(jnp.swapaxes(cp, 0, 1))
            conv_s.append(cs)
            h_p.append(hp)
            h_s.append(hs)
        else:
            o = layer // 2
            xp, pp = _pool_prompt(xp, pool_params, o)
            xs, ps = _pool_sample(xs, state_pool, pool_params, o)
            pool_p.append(jnp.swapaxes(pp, 0, 1))
            pool_s.append(ps)
        last = layer == depth - 1
        xp = _mlp(xp, g_ffn, w1, w2, g_fin, layer, tm=MLP_TM, final_norm=last,
                  batch_major_out=last)
        xs = _mlp(xs, g_ffn, w1, w2, g_fin, layer, tm=n_s, final_norm=last)
    return (xp, xs.reshape(n_s, 1, D_MODEL), jnp.stack(sgu_v_s), jnp.stack(conv_p),
            jnp.stack(conv_s), jnp.stack(h_p), jnp.stack(h_s), jnp.stack(pool_p),
            jnp.stack(pool_s))
```

```python
import functools
import math

import jax
import jax.numpy as jnp
from jax import lax
from jax.experimental import pallas as pl
from jax.experimental.pallas import tpu as pltpu

D_MODEL = 1024
CHUNK = 128
W_A = D_MODEL // 2
H_A = 8
DH_A = W_A // H_A
W_B = D_MODEL // 2
H_B = 8
DH_B = W_B // H_B
CONV_W = 4
LRU_C = 8.0
POOL_WINDOWS = (2, 4, 8, 16)
N_POOL = len(POOL_WINDOWS)
G_POOL = D_MODEL // N_POOL
POOL_HIST = max(POOL_WINDOWS) - 1
D_FF = 4 * D_MODEL
EPS = 1e-6
PAST_LEN = 16384

BF16 = jnp.bfloat16
F32 = jnp.float32

VMEM_LIMIT_BYTES = 56 * 1024 * 1024
LANES = 128
NB = 8
MXU_K = 256
POOL_T = 128
MLP_TM = 1024
FF_CHUNK = 1024


def _rms(x, g):
    return x * lax.rsqrt(jnp.mean(x * x, axis=-1, keepdims=True) + EPS) * g


def _dot(a, b):
    return jnp.dot(a, b, preferred_element_type=F32)


_GELU_K1 = -2.0 * math.sqrt(2.0 / math.pi) * math.log2(math.e)
_GELU_K2 = _GELU_K1 * 0.044715


def _gelu(x):
    return x / (1.0 + jnp.exp2(x * (_GELU_K1 + _GELU_K2 * (x * x))))


def _sigmoid(x):
    return 0.5 * jnp.tanh(0.5 * x) + 0.5


def _softplus(z):
    return jnp.maximum(z, 0.0) + jnp.log1p(jnp.exp(-jnp.abs(z)))


def _const_spec(shape):
    nd = len(shape)
    return pl.BlockSpec(shape, lambda *_: (0,) * nd)


def _layer_spec(arr, layer):
    nd = arr.ndim - 1
    return pl.BlockSpec((None,) + arr.shape[1:], lambda *_: (layer,) + (0,) * nd)


def _params(n_axes):
    return pltpu.CompilerParams(
        dimension_semantics=("arbitrary",) * n_axes,
        vmem_limit_bytes=VMEM_LIMIT_BYTES)


def _lru_coeffs(xc, wg_ref, bg_ref, lam_ref):
    xcb = xc.astype(BF16)
    rs, is_ = [], []
    for kb in range(W_B // MXU_K):
        gk = _dot(xcb[:, kb * MXU_K:(kb + 1) * MXU_K], wg_ref[kb])
        rs.append(gk[:, :MXU_K])
        is_.append(gk[:, MXU_K:])
    bg = bg_ref[...]
    r = _sigmoid(jnp.concatenate(rs, axis=1) + bg[:, :W_B])
    i = _sigmoid(jnp.concatenate(is_, axis=1) + bg[:, W_B:])
    log_a = r * (-LRU_C * _softplus(-lam_ref[...]))
    a = jnp.exp(log_a)
    t = jnp.tanh(log_a)
    n = -2.0 * t
    mult = jnp.where(n > 0.0, n * lax.rsqrt(n * (1.0 - t)), 0.0)
    return a, mult * (i * xc)


def _even_prompt_kernel(x_ref, g_ref, win_ref, wout_ref, vn_ref, sw_ref, sb_ref,
                        cw_ref, cb_ref, wg_ref, bg_ref, lam_ref,
                        xo_ref, conv_ref, h_ref,
                        xb_buf, v_buf, s_buf, a_buf, b_buf, h_carry, *maybe_xt_buf,
                        batch_major_in):
    rows = NB * CHUNK
    hist_rows = (CONV_W - 1) * NB
    j = pl.program_id(0)

    @pl.when(j == 0)
    def _():
        xb_buf[0:hist_rows, :] = jnp.zeros((hist_rows, W_B), F32)
        h_carry[...] = jnp.zeros_like(h_carry)

    if batch_major_in:
        xt_buf, = maybe_xt_buf
        for b in range(NB):
            for c in range(D_MODEL // LANES):
                xt_buf[c, pl.ds(b, CHUNK, stride=NB), :] = x_ref[b, :, c * LANES:(c + 1) * LANES]
        x = jnp.concatenate([xt_buf[c] for c in range(D_MODEL // LANES)], axis=1)
    else:
        x = x_ref[...]
    xn = _rms(x, g_ref[...]).astype(BF16)
    proj_a = _dot(xn, win_ref[:, 0:2 * W_A])
    u = _gelu(proj_a[:, :W_A])
    v = _rms(_gelu(proj_a[:, W_A:]), vn_ref[...])
    for p in range(W_A // LANES):
        v_buf[p] = v[:, p * LANES:(p + 1) * LANES]

    lane = lax.broadcasted_iota(jnp.int32, (CHUNK, LANES), 1)
    zero = jnp.zeros((CHUNK, LANES), BF16)
    for p in range(W_A // LANES):
        los, his = [], []
        for b in range(NB):
            vb = v_buf[p, pl.ds(b, CHUNK, stride=NB), :].astype(BF16)
            los.append(jnp.where(lane < DH_A, vb, zero))
            his.append(jnp.where(lane >= DH_A, vb, zero))
        rhs = jnp.concatenate([jnp.concatenate(los, axis=1),
                               jnp.concatenate(his, axis=1)], axis=0)
        o = _dot(sw_ref[p], rhs)
        sbp = sb_ref[:, p * LANES:(p + 1) * LANES]
        for b in range(NB):
            s_buf[p, pl.ds(b, CHUNK, stride=NB), :] = o[:, b * LANES:(b + 1) * LANES] + sbp
    s = jnp.concatenate([s_buf[p] for p in range(W_A // LANES)], axis=1)
    a_out = (u * s).astype(BF16)

    proj_b = _dot(xn, win_ref[:, 2 * W_A:])
    gate_act = _gelu(proj_b[:, :W_B])
    xb_buf[hist_rows:hist_rows + rows, :] = proj_b[:, W_B:]

    cw = cw_ref[...]
    xc = cb_ref[...] + sum(xb_buf[k * NB:k * NB + rows, :] * cw[k:k + 1, :]
                           for k in range(CONV_W))
    tail = xb_buf[rows:rows + hist_rows, :]
    xb_buf[0:hist_rows, :] = tail

    a, bt = _lru_coeffs(xc, wg_ref, bg_ref, lam_ref)
    a_buf[...] = a
    b_buf[...] = bt

    def step(t, h):
        sl = pl.ds(pl.multiple_of(t * NB, NB), NB)
        h = a_buf[sl, :] * h + b_buf[sl, :]
        b_buf[sl, :] = h
        return h

    h_last = lax.fori_loop(0, CHUNK, step, h_carry[...], unroll=8)
    h_carry[...] = h_last

    b_out = (b_buf[...] * gate_act).astype(BF16)
    y = _dot(jnp.concatenate([a_out, b_out], axis=1), wout_ref[...])
    xo_ref[...] = x + y

    @pl.when(j == pl.num_programs(0) - 1)
    def _():
        conv_ref[...] = tail.reshape(CONV_W - 1, NB, W_B)
        h_ref[...] = h_last


def _even_prompt(x, layer_params, e, *, batch_major_in):
    seq = x.shape[1] if batch_major_in else x.shape[0] // NB
    rows = NB * CHUNK
    if batch_major_in:
        x_spec = pl.BlockSpec((NB, CHUNK, D_MODEL), lambda j: (0, j, 0))
    else:
        x_spec = pl.BlockSpec((rows, D_MODEL), lambda j: (j, 0))
    slab = lambda width: pltpu.VMEM((width // LANES, rows, LANES), F32)
    scratch = [pltpu.VMEM(((CONV_W - 1) * NB + rows, W_B), F32),
               slab(W_A), slab(W_A),
               pltpu.VMEM((rows, W_B), F32), pltpu.VMEM((rows, W_B), F32),
               pltpu.VMEM((NB, W_B), F32)]
    if batch_major_in:
        scratch.append(slab(D_MODEL))
    return pl.pallas_call(
        functools.partial(_even_prompt_kernel, batch_major_in=batch_major_in),
        grid=(seq // CHUNK,),
        in_specs=[x_spec] + [_layer_spec(p, e) for p in layer_params],
        out_specs=[pl.BlockSpec((rows, D_MODEL), lambda j: (j, 0)),
                   _const_spec((CONV_W - 1, NB, W_B)),
                   _const_spec((NB, W_B))],
        out_shape=[jax.ShapeDtypeStruct((seq * NB, D_MODEL), F32),
                   jax.ShapeDtypeStruct((CONV_W - 1, NB, W_B), F32),
                   jax.ShapeDtypeStruct((NB, W_B), F32)],
        scratch_shapes=scratch,
        compiler_params=_params(1),
        name="even_prompt",
    )(x, *layer_params)


def _even_sample_kernel(x_ref, conv_ref, h0_ref, g_ref, win_ref, wout_ref, vn_ref,
                        w00_ref, b0_ref, cw_ref, cb_ref, wg_ref, bg_ref, lam_ref,
                        xo_ref, v_ref, convo_ref, ho_ref):
    x = x_ref[...]
    xn = _rms(x, g_ref[...]).astype(BF16)
    proj = _dot(xn, win_ref[...])
    u = _gelu(proj[:, :W_A])
    v = _rms(_gelu(proj[:, W_A:2 * W_A]), vn_ref[...])
    gate = proj[:, 2 * W_A:2 * W_A + W_B]
    xb = proj[:, 2 * W_A + W_B:]
    v_ref[...] = v
    a_out = (u * (v * w00_ref[...] + b0_ref[...])).astype(BF16)

    cw = cw_ref[...]
    xc = cb_ref[...] + xb * cw[CONV_W - 1:CONV_W, :]
    for k in range(CONV_W - 1):
        hk = conv_ref[k]
        xc = xc + hk * cw[k:k + 1, :]
        if k > 0:
            convo_ref[k - 1] = hk
    convo_ref[CONV_W - 2] = xb

    a, bt = _lru_coeffs(xc, wg_ref, bg_ref, lam_ref)
    h = a * h0_ref[...] + bt
    ho_ref[...] = h
    b_out = (h * _gelu(gate)).astype(BF16)
    y = _dot(jnp.concatenate([a_out, b_out], axis=1), wout_ref[...])
    xo_ref[...] = x + y


def _even_sample(x, state_conv, state_h, layer_params, e):
    n = x.shape[0]
    return pl.pallas_call(
        _even_sample_kernel,
        grid=(1,),
        in_specs=[_const_spec(x.shape), _layer_spec(state_conv, e), _layer_spec(state_h, e)]
                 + [_layer_spec(p, e) for p in layer_params],
        out_specs=[_const_spec((n, D_MODEL)), _const_spec((n, W_A)),
                   _const_spec(state_conv.shape[1:]), _const_spec((n, W_B))],
        out_shape=[jax.ShapeDtypeStruct((n, D_MODEL), F32),
                   jax.ShapeDtypeStruct((n, W_A), F32),
                   jax.ShapeDtypeStruct(state_conv.shape[1:], F32),
                   jax.ShapeDtypeStruct((n, W_B), F32)],
        compiler_params=_params(1),
        name="even_sample",
    )(x, state_conv, state_h, *layer_params)


def _pool_prompt_kernel(x_ref, g_ref, wp_ref, bp_ref, sc_ref, xo_ref, hist_ref, z_buf):
    rows = x_ref.shape[0]
    t_steps = rows // NB
    pad = (POOL_HIST + 1) * NB
    j = pl.program_id(0)

    @pl.when(j == 0)
    def _():
        z_buf[0:pad, :] = jnp.zeros((pad, D_MODEL), F32)

    x = x_ref[...]
    xn = _rms(x, g_ref[...])
    z_buf[pad:pad + rows, :] = xn
    row = lax.broadcasted_iota(jnp.int32, (rows, 1), 0)
    pos = j * t_steps + jnp.right_shift(row, NB.bit_length() - 1)
    ys = []
    for gi, w in enumerate(POOL_WINDOWS):
        c0 = gi * G_POOL
        acc = z_buf[pad - (w - 1) * NB:pad + rows, c0:c0 + G_POOL]
        span = 1
        while span < w:
            acc = acc[span * NB:, :] + acc[:-span * NB, :]
            span *= 2
        cnt = jnp.minimum(pos + 1, w).astype(F32)
        p = (acc / cnt - xn[:, c0:c0 + G_POOL]).astype(BF16)
        ys.append(_dot(p, wp_ref[gi]))
    y = (jnp.concatenate(ys, axis=1) + bp_ref[...]) * sc_ref[...]
    xo_ref[...] = x + y
    tail = z_buf[rows + NB:rows + pad, :]
    z_buf[NB:pad, :] = tail

    @pl.when(j == pl.num_programs(0) - 1)
    def _():
        hist_ref[...] = tail.reshape(POOL_HIST, NB, D_MODEL)


def _pool_prompt(x, layer_params, o):
    n_rows = x.shape[0]
    rows = POOL_T * NB
    x_spec = pl.BlockSpec((rows, D_MODEL), lambda j: (j, 0))
    return pl.pallas_call(
        _pool_prompt_kernel,
        grid=(n_rows // rows,),
        in_specs=[x_spec] + [_layer_spec(p, o) for p in layer_params],
        out_specs=[x_spec, _const_spec((POOL_HIST, NB, D_MODEL))],
        out_shape=[jax.ShapeDtypeStruct(x.shape, F32),
                   jax.ShapeDtypeStruct((POOL_HIST, NB, D_MODEL), F32)],
        scratch_shapes=[pltpu.VMEM(((POOL_HIST + 1) * NB + rows, D_MODEL), F32)],
        compiler_params=_params(1),
        name="pool_prompt",
    )(x, *layer_params)


def _pool_sample_kernel(x_ref, hist_ref, g_ref, wp_ref, bp_ref, sc_ref, xo_ref, histo_ref):
    x = x_ref[...]
    xn = _rms(x, g_ref[...])
    ys = []
    for gi, w in enumerate(POOL_WINDOWS):
        c0 = gi * G_POOL
        xg = xn[:, c0:c0 + G_POOL]
        wsum = xg
        for d in range(1, w):
            wsum = wsum + hist_ref[POOL_HIST - d, :, c0:c0 + G_POOL]
        cnt = float(min(PAST_LEN + 1, w))
        p = (wsum / cnt - xg).astype(BF16)
        ys.append(_dot(p, wp_ref[gi]))
    y = (jnp.concatenate(ys, axis=1) + bp_ref[...]) * sc_ref[...]
    xo_ref[...] = x + y
    histo_ref[0:POOL_HIST - 1] = hist_ref[1:POOL_HIST]
    histo_ref[POOL_HIST - 1] = xn


def _pool_sample(x, state_pool, layer_params, o):
    n = x.shape[0]
    return pl.pallas_call(
        _pool_sample_kernel,
        grid=(1,),
        in_specs=[_const_spec(x.shape), _layer_spec(state_pool, o)]
                 + [_layer_spec(p, o) for p in layer_params],
        out_specs=[_const_spec((n, D_MODEL)), _const_spec(state_pool.shape[1:])],
        out_shape=[jax.ShapeDtypeStruct((n, D_MODEL), F32),
                   jax.ShapeDtypeStruct(state_pool.shape[1:], F32)],
        compiler_params=_params(1),
        name="pool_sample",
    )(x, state_pool, *layer_params)


def _mlp_kernel(x_ref, g_ref, w1_ref, w2_ref, gf_ref, o_ref, *maybe_ot_buf,
                final_norm, batch_major_out):
    x = x_ref[...]
    xn = _rms(x, g_ref[...]).astype(BF16)
    acc = x
    for c in range(D_FF // FF_CHUNK):
        h = _dot(xn, w1_ref[:, c * FF_CHUNK:(c + 1) * FF_CHUNK])
        h = jnp.square(jnp.maximum(h, 0.0)).astype(BF16)
        acc = acc + _dot(h, w2_ref[c * FF_CHUNK:(c + 1) * FF_CHUNK, :])
    if final_norm:
        acc = _rms(acc, gf_ref[...])
    if batch_major_out:
        ot_buf, = maybe_ot_buf
        t_steps = x.shape[0] // NB
        for c in range(D_MODEL // LANES):
            ot_buf[c] = acc[:, c * LANES:(c + 1) * LANES]
        for b in range(NB):
            for c in range(D_MODEL // LANES):
                o_ref[b, :, c * LANES:(c + 1) * LANES] = ot_buf[c, pl.ds(b, t_steps, stride=NB), :]
    else:
        o_ref[...] = acc


def _mlp(x, g, w1, w2, gf, layer, *, tm, final_norm, batch_major_out=False):
    n = x.shape[0]
    x_spec = pl.BlockSpec((tm, D_MODEL), lambda i: (i, 0))
    if batch_major_out:
        t_steps = tm // NB
        out_spec = pl.BlockSpec((NB, t_steps, D_MODEL), lambda i: (0, i, 0))
        out_shape = jax.ShapeDtypeStruct((NB, n // NB, D_MODEL), F32)
        scratch = [pltpu.VMEM((D_MODEL // LANES, tm, LANES), F32)]
    else:
        out_spec, out_shape, scratch = x_spec, jax.ShapeDtypeStruct(x.shape, F32), []
    return pl.pallas_call(
        functools.partial(_mlp_kernel, final_norm=final_norm, batch_major_out=batch_major_out),
        grid=(n // tm,),
        in_specs=[x_spec, _layer_spec(g, layer), _layer_spec(w1, layer), _layer_spec(w2, layer),
                  _const_spec(gf.shape)],
        out_specs=out_spec,
        out_shape=out_shape,
        scratch_shapes=scratch,
        compiler_params=_params(1),
        name="mlp",
    )(x, g, w1, w2, gf)


def _gate_blocks(wa, wx):
    n_e = wa.shape[0]
    hpb = MXU_K // DH_B
    eye = jnp.eye(hpb, dtype=wa.dtype)

    def bd(w):
        w = w.reshape(n_e, H_B // hpb, hpb, DH_B, DH_B)
        return jnp.einsum('eghij,hk->eghikj', w, eye).reshape(n_e, H_B // hpb, MXU_K, MXU_K)

    return jnp.concatenate([bd(wa), bd(wx)], axis=-1)


def _rows(v):
    return v.reshape(v.shape[0], 1, v.shape[1])


def kernel(x_prompt, x_sample, state_conv, state_rglru, state_pool, norm_mix, norm_ffn, norm_final, w_in, w_out, v_norm, sgu_w, sgu_b, conv_w, conv_b, gate_a_w, gate_a_b, gate_x_w, gate_x_b, lru_lambda, pool_w, pool_b, pool_scale, ffn_w1, ffn_w2):
    n_p, seq, _ = x_prompt.shape
    n_s = x_sample.shape[0]
    depth = norm_mix.shape[0]
    assert n_p == NB and seq % CHUNK == 0

    tril = jnp.tril(jnp.ones((CHUNK, CHUNK), dtype=bool))
    wm = jnp.where(tril, sgu_w, 0.0)
    sw = jnp.concatenate([wm[:, 0::2], wm[:, 1::2]], axis=3).astype(BF16)
    sb = jnp.repeat(jnp.swapaxes(sgu_b, 1, 2), DH_A, axis=2)
    wg = _gate_blocks(gate_a_w, gate_x_w).astype(BF16)
    bg = _rows(jnp.concatenate([gate_a_b, gate_x_b], axis=1))
    w00 = _rows(jnp.repeat(wm[:, :, 0, 0], DH_A, axis=1))
    b0 = _rows(jnp.repeat(sgu_b[:, :, 0], DH_A, axis=1))
    win, wout = w_in.astype(BF16), w_out.astype(BF16)
    w1, w2 = ffn_w1.astype(BF16), ffn_w2.astype(BF16)
    wp = pool_w.astype(BF16)
    g_mix_even, g_mix_odd = _rows(norm_mix[0::2]), _rows(norm_mix[1::2])
    g_ffn = _rows(norm_ffn)
    g_fin = norm_final.reshape(1, D_MODEL)
    even_tail = (conv_w, _rows(conv_b), wg, bg, _rows(lru_lambda))
    even_p = (g_mix_even, win, wout, _rows(v_norm), sw, sb) + even_tail
    even_s = (g_mix_even, win, wout, _rows(v_norm), w00, b0) + even_tail
    pool_params = (g_mix_odd, wp, _rows(pool_b), _rows(pool_scale))

    conv_state = jnp.swapaxes(state_conv, 1, 2)
    pool_state = jnp.swapaxes(state_pool, 1, 2)

    xp = x_prompt
    xs = x_sample.reshape(n_s, D_MODEL)
    sgu_v_s, conv_p, conv_s, h_p, h_s, pool_p, pool_s = [], [], [], [], [], [], []
    for layer in range(depth):
        if layer % 2 == 0:
            e = layer // 2
            xp, cp, hp = _even_prompt(xp, even_p, e, batch_major_in=(layer == 0))
            xs, vs, cs, hs = _even_sample(xs, conv_state, state_rglru, even_s, e)
            sgu_v_s.append(vs.reshape(n_s, 1, W_A))
            conv_p.append(cp)
            conv_s.append(cs)
            h_p.append(hp)
            h_s.append(hs)
        else:
            o = layer // 2
            xp, pp = _pool_prompt(xp, pool_params, o)
            xs, ps = _pool_sample(xs, pool_state, pool_params, o)
            pool_p.append(pp)
            pool_s.append(ps)
        last = layer == depth - 1
        xp = _mlp(xp, g_ffn, w1, w2, g_fin, layer, tm=MLP_TM, final_norm=last,
                  batch_major_out=last)
        xs = _mlp(xs, g_ffn, w1, w2, g_fin, layer, tm=n_s, final_norm=last)

    def stack_positions(parts):
        return jnp.swapaxes(jnp.stack(parts), 1, 2)

    return (xp, xs.reshape(n_s, 1, D_MODEL), jnp.stack(sgu_v_s), stack_positions(conv_p),
            stack_positions(conv_s), jnp.stack(h_p), jnp.stack(h_s), stack_positions(pool_p),
            stack_positions(pool_s))
```

```python
import functools
import math

import jax
import jax.numpy as jnp
from jax import lax
from jax.experimental import pallas as pl
from jax.experimental.pallas import tpu as pltpu

D_MODEL = 1024
CHUNK = 128
W_A = D_MODEL // 2
H_A = 8
DH_A = W_A // H_A
W_B = D_MODEL // 2
H_B = 8
DH_B = W_B // H_B
CONV_W = 4
LRU_C = 8.0
POOL_WINDOWS = (2, 4, 8, 16)
N_POOL = len(POOL_WINDOWS)
G_POOL = D_MODEL // N_POOL
POOL_HIST = max(POOL_WINDOWS) - 1
D_FF = 4 * D_MODEL
EPS = 1e-6
PAST_LEN = 16384

BF16 = jnp.bfloat16
F32 = jnp.float32

VMEM_LIMIT_BYTES = 56 * 1024 * 1024
LANES = 128
NB = 8
MXU_K = 256
EVEN_RB = 512
POOL_T = 128
MLP_TM = 1024
FF_CHUNK = 1024


def _rms(x, g):
    return x * lax.rsqrt(jnp.mean(x * x, axis=-1, keepdims=True) + EPS) * g


def _dot(a, b):
    return jnp.dot(a, b, preferred_element_type=F32)


_GELU_K1 = -2.0 * math.sqrt(2.0 / math.pi) * math.log2(math.e)
_GELU_K2 = _GELU_K1 * 0.044715


def _gelu(x):
    return x / (1.0 + jnp.exp2(x * (_GELU_K1 + _GELU_K2 * (x * x))))


def _sigmoid(x):
    return 0.5 * jnp.tanh(0.5 * x) + 0.5


def _softplus(z):
    return jnp.maximum(z, 0.0) + jnp.log1p(jnp.exp(-jnp.abs(z)))


def _const_spec(shape):
    nd = len(shape)
    return pl.BlockSpec(shape, lambda *_: (0,) * nd)


def _layer_spec(arr, layer):
    nd = arr.ndim - 1
    return pl.BlockSpec((None,) + arr.shape[1:], lambda *_: (layer,) + (0,) * nd)


def _params(n_axes):
    return pltpu.CompilerParams(
        dimension_semantics=("arbitrary",) * n_axes,
        vmem_limit_bytes=VMEM_LIMIT_BYTES)


def _lru_coeffs(xc, wg_ref, bg_ref, lam_ref):
    xcb = xc.astype(BF16)
    rs, is_ = [], []
    for kb in range(W_B // MXU_K):
        gk = _dot(xcb[:, kb * MXU_K:(kb + 1) * MXU_K], wg_ref[kb])
        rs.append(gk[:, :MXU_K])
        is_.append(gk[:, MXU_K:])
    bg = bg_ref[...]
    r = _sigmoid(jnp.concatenate(rs, axis=1) + bg[:, :W_B])
    i = _sigmoid(jnp.concatenate(is_, axis=1) + bg[:, W_B:])
    log_a = r * (-LRU_C * _softplus(-lam_ref[...]))
    a = jnp.exp(log_a)
    t = jnp.tanh(log_a)
    n = -2.0 * t
    mult = jnp.where(n > 0.0, n * lax.rsqrt(n * (1.0 - t)), 0.0)
    return a, mult * (i * xc)


def _even_prompt_kernel(x_ref, g_ref, win_ref, wout_ref, vn_ref, sw_ref, sb_ref,
                        cw_ref, cb_ref, wg_ref, bg_ref, lam_ref,
                        xo_ref, conv_ref, h_ref,
                        xb_buf, v_buf, s_buf, xn_buf, u_buf, h_carry, *maybe_xt_buf,
                        batch_major_in):
    rows = NB * CHUNK
    hist_rows = (CONV_W - 1) * NB
    n_rb = rows // EVEN_RB
    t_rb = EVEN_RB // NB
    n_xcol = D_MODEL // LANES
    j = pl.program_id(0)

    @pl.when(j == 0)
    def _():
        xb_buf[0:hist_rows, :] = jnp.zeros((hist_rows, W_B), F32)
        h_carry[...] = jnp.zeros_like(h_carry)

    def load_x(r0):
        if batch_major_in:
            xt_buf, = maybe_xt_buf
            return jnp.concatenate([xt_buf[c, r0:r0 + EVEN_RB, :] for c in range(n_xcol)], axis=1)
        return x_ref[r0:r0 + EVEN_RB, :]

    for k in range(n_rb):
        r0 = k * EVEN_RB
        if batch_major_in:
            xt_buf, = maybe_xt_buf
            for b in range(NB):
                for c in range(n_xcol):
                    xt_buf[c, pl.ds(r0 + b, t_rb, stride=NB), :] = (
                        x_ref[b, k * t_rb:(k + 1) * t_rb, c * LANES:(c + 1) * LANES])
        xn = _rms(load_x(r0), g_ref[...]).astype(BF16)
        xn_buf[r0:r0 + EVEN_RB, :] = xn
        proj_a = _dot(xn, win_ref[:, 0:2 * W_A])
        u_buf[r0:r0 + EVEN_RB, :] = _gelu(proj_a[:, :W_A])
        v = _rms(_gelu(proj_a[:, W_A:]), vn_ref[...])
        for p in range(W_A // LANES):
            v_buf[p, r0:r0 + EVEN_RB, :] = v[:, p * LANES:(p + 1) * LANES]

    lane = lax.broadcasted_iota(jnp.int32, (CHUNK, LANES), 1)
    zero = jnp.zeros((CHUNK, LANES), BF16)
    for p in range(W_A // LANES):
        los, his = [], []
        for b in range(NB):
            vb = v_buf[p, pl.ds(b, CHUNK, stride=NB), :].astype(BF16)
            los.append(jnp.where(lane < DH_A, vb, zero))
            his.append(jnp.where(lane >= DH_A, vb, zero))
        rhs = jnp.concatenate([jnp.concatenate(los, axis=1),
                               jnp.concatenate(his, axis=1)], axis=0)
        o = _dot(sw_ref[p], rhs)
        sbp = sb_ref[:, p * LANES:(p + 1) * LANES]
        for b in range(NB):
            s_buf[p, pl.ds(b, CHUNK, stride=NB), :] = o[:, b * LANES:(b + 1) * LANES] + sbp

    cw = cw_ref[...]
    h = h_carry[...]
    def proj_b_block(k):
        return _dot(xn_buf[k * EVEN_RB:(k + 1) * EVEN_RB, :], win_ref[:, 2 * W_A:])

    proj_b_next = proj_b_block(0)
    for k in range(n_rb):
        r0 = k * EVEN_RB
        proj_b = proj_b_next
        if k + 1 < n_rb:
            proj_b_next = proj_b_block(k + 1)
        gate_act = _gelu(proj_b[:, :W_B])
        xb_buf[hist_rows + r0:hist_rows + r0 + EVEN_RB, :] = proj_b[:, W_B:]
        xc = cb_ref[...] + sum(xb_buf[r0 + kk * NB:r0 + kk * NB + EVEN_RB, :] * cw[kk:kk + 1, :]
                               for kk in range(CONV_W))
        a, bt = _lru_coeffs(xc, wg_ref, bg_ref, lam_ref)
        hs = []
        for t in range(t_rb):
            h = a[t * NB:(t + 1) * NB, :] * h + bt[t * NB:(t + 1) * NB, :]
            hs.append(h)
        b_out = (jnp.concatenate(hs, axis=0) * gate_act).astype(BF16)
        s = jnp.concatenate([s_buf[p, r0:r0 + EVEN_RB, :] for p in range(W_A // LANES)], axis=1)
        a_out = (u_buf[r0:r0 + EVEN_RB, :] * s).astype(BF16)
        y = _dot(jnp.concatenate([a_out, b_out], axis=1), wout_ref[...])
        xo_ref[r0:r0 + EVEN_RB, :] = load_x(r0) + y
    h_carry[...] = h
    tail = xb_buf[rows:rows + hist_rows, :]
    xb_buf[0:hist_rows, :] = tail

    @pl.when(j == pl.num_programs(0) - 1)
    def _():
        conv_ref[...] = tail.reshape(CONV_W - 1, NB, W_B)
        h_ref[...] = h


def _even_prompt(x, layer_params, e, *, batch_major_in):
    seq = x.shape[1] if batch_major_in else x.shape[0] // NB
    rows = NB * CHUNK
    if batch_major_in:
        x_spec = pl.BlockSpec((NB, CHUNK, D_MODEL), lambda j: (0, j, 0))
    else:
        x_spec = pl.BlockSpec((rows, D_MODEL), lambda j: (j, 0))
    slab = lambda width: pltpu.VMEM((width // LANES, rows, LANES), F32)
    scratch = [pltpu.VMEM(((CONV_W - 1) * NB + rows, W_B), F32),
               slab(W_A), slab(W_A),
               pltpu.VMEM((rows, D_MODEL), BF16), pltpu.VMEM((rows, W_A), F32),
               pltpu.VMEM((NB, W_B), F32)]
    if batch_major_in:
        scratch.append(slab(D_MODEL))
    return pl.pallas_call(
        functools.partial(_even_prompt_kernel, batch_major_in=batch_major_in),
        grid=(seq // CHUNK,),
        in_specs=[x_spec] + [_layer_spec(p, e) for p in layer_params],
        out_specs=[pl.BlockSpec((rows, D_MODEL), lambda j: (j, 0)),
                   _const_spec((CONV_W - 1, NB, W_B)),
                   _const_spec((NB, W_B))],
        out_shape=[jax.ShapeDtypeStruct((seq * NB, D_MODEL), F32),
                   jax.ShapeDtypeStruct((CONV_W - 1, NB, W_B), F32),
                   jax.ShapeDtypeStruct((NB, W_B), F32)],
        scratch_shapes=scratch,
        compiler_params=_params(1),
        name="even_prompt",
    )(x, *layer_params)


def _even_sample_kernel(x_ref, conv_ref, h0_ref, g_ref, win_ref, wout_ref, vn_ref,
                        w00_ref, b0_ref, cw_ref, cb_ref, wg_ref, bg_ref, lam_ref,
                        xo_ref, v_ref, convo_ref, ho_ref):
    x = x_ref[...]
    xn = _rms(x, g_ref[...]).astype(BF16)
    proj = _dot(xn, win_ref[...])
    u = _gelu(proj[:, :W_A])
    v = _rms(_gelu(proj[:, W_A:2 * W_A]), vn_ref[...])
    gate = proj[:, 2 * W_A:2 * W_A + W_B]
    xb = proj[:, 2 * W_A + W_B:]
    v_ref[...] = v
    a_out = (u * (v * w00_ref[...] + b0_ref[...])).astype(BF16)

    cw = cw_ref[...]
    xc = cb_ref[...] + xb * cw[CONV_W - 1:CONV_W, :]
    for k in range(CONV_W - 1):
        hk = conv_ref[k]
        xc = xc + hk * cw[k:k + 1, :]
        if k > 0:
            convo_ref[k - 1] = hk
    convo_ref[CONV_W - 2] = xb

    a, bt = _lru_coeffs(xc, wg_ref, bg_ref, lam_ref)
    h = a * h0_ref[...] + bt
    ho_ref[...] = h
    b_out = (h * _gelu(gate)).astype(BF16)
    y = _dot(jnp.concatenate([a_out, b_out], axis=1), wout_ref[...])
    xo_ref[...] = x + y


def _even_sample(x, state_conv, state_h, layer_params, e):
    n = x.shape[0]
    return pl.pallas_call(
        _even_sample_kernel,
        grid=(1,),
        in_specs=[_const_spec(x.shape), _layer_spec(state_conv, e), _layer_spec(state_h, e)]
                 + [_layer_spec(p, e) for p in layer_params],
        out_specs=[_const_spec((n, D_MODEL)), _const_spec((n, W_A)),
                   _const_spec(state_conv.shape[1:]), _const_spec((n, W_B))],
        out_shape=[jax.ShapeDtypeStruct((n, D_MODEL), F32),
                   jax.ShapeDtypeStruct((n, W_A), F32),
                   jax.ShapeDtypeStruct(state_conv.shape[1:], F32),
                   jax.ShapeDtypeStruct((n, W_B), F32)],
        compiler_params=_params(1),
        name="even_sample",
    )(x, state_conv, state_h, *layer_params)


def _pool_prompt_kernel(x_ref, g_ref, wp_ref, bp_ref, sc_ref, xo_ref, hist_ref, z_buf):
    rows = x_ref.shape[0]
    t_steps = rows // NB
    pad = (POOL_HIST + 1) * NB
    j = pl.program_id(0)

    @pl.when(j == 0)
    def _():
        z_buf[0:pad, :] = jnp.zeros((pad, D_MODEL), F32)

    x = x_ref[...]
    xn = _rms(x, g_ref[...])
    z_buf[pad:pad + rows, :] = xn
    row = lax.broadcasted_iota(jnp.int32, (rows, 1), 0)
    pos = j * t_steps + jnp.right_shift(row, NB.bit_length() - 1)
    ys = []
    for gi, w in enumerate(POOL_WINDOWS):
        c0 = gi * G_POOL
        acc = z_buf[pad - (w - 1) * NB:pad + rows, c0:c0 + G_POOL]
        span = 1
        while span < w:
            acc = acc[span * NB:, :] + acc[:-span * NB, :]
            span *= 2
        cnt = jnp.minimum(pos + 1, w).astype(F32)
        p = (acc / cnt - xn[:, c0:c0 + G_POOL]).astype(BF16)
        ys.append(_dot(p, wp_ref[gi]))
    y = (jnp.concatenate(ys, axis=1) + bp_ref[...]) * sc_ref[...]
    xo_ref[...] = x + y
    tail = z_buf[rows + NB:rows + pad, :]
    z_buf[NB:pad, :] = tail

    @pl.when(j == pl.num_programs(0) - 1)
    def _():
        hist_ref[...] = tail.reshape(POOL_HIST, NB, D_MODEL)


def _pool_prompt(x, layer_params, o):
    n_rows = x.shape[0]
    rows = POOL_T * NB
    x_spec = pl.BlockSpec((rows, D_MODEL), lambda j: (j, 0))
    return pl.pallas_call(
        _pool_prompt_kernel,
        grid=(n_rows // rows,),
        in_specs=[x_spec] + [_layer_spec(p, o) for p in layer_params],
        out_specs=[x_spec, _const_spec((POOL_HIST, NB, D_MODEL))],
        out_shape=[jax.ShapeDtypeStruct(x.shape, F32),
                   jax.ShapeDtypeStruct((POOL_HIST, NB, D_MODEL), F32)],
        scratch_shapes=[pltpu.VMEM(((POOL_HIST + 1) * NB + rows, D_MODEL), F32)],
        compiler_params=_params(1),
        name="pool_prompt",
    )(x, *layer_params)


def _pool_sample_kernel(x_ref, hist_ref, g_ref, wp_ref, bp_ref, sc_ref, xo_ref, histo_ref):
    x = x_ref[...]
    xn = _rms(x, g_ref[...])
    ys = []
    for gi, w in enumerate(POOL_WINDOWS):
        c0 = gi * G_POOL
        xg = xn[:, c0:c0 + G_POOL]
        wsum = xg
        for d in range(1, w):
            wsum = wsum + hist_ref[POOL_HIST - d, :, c0:c0 + G_POOL]
        cnt = float(min(PAST_LEN + 1, w))
        p = (wsum / cnt - xg).astype(BF16)
        ys.append(_dot(p, wp_ref[gi]))
    y = (jnp.concatenate(ys, axis=1) + bp_ref[...]) * sc_ref[...]
    xo_ref[...] = x + y
    histo_ref[0:POOL_HIST - 1] = hist_ref[1:POOL_HIST]
    histo_ref[POOL_HIST - 1] = xn


def _pool_sample(x, state_pool, layer_params, o):
    n = x.shape[0]
    return pl.pallas_call(
        _pool_sample_kernel,
        grid=(1,),
        in_specs=[_const_spec(x.shape), _layer_spec(state_pool, o)]
                 + [_layer_spec(p, o) for p in layer_params],
        out_specs=[_const_spec((n, D_MODEL)), _const_spec(state_pool.shape[1:])],
        out_shape=[jax.ShapeDtypeStruct((n, D_MODEL), F32),
                   jax.ShapeDtypeStruct(state_pool.shape[1:], F32)],
        compiler_params=_params(1),
        name="pool_sample",
    )(x, state_pool, *layer_params)


def _mlp_kernel(x_ref, g_ref, w1_ref, w2_ref, gf_ref, o_ref, *maybe_ot_buf,
                final_norm, batch_major_out):
    x = x_ref[...]
    xn = _rms(x, g_ref[...]).astype(BF16)
    acc = x
    for c in range(D_FF // FF_CHUNK):
        h = _dot(xn, w1_ref[:, c * FF_CHUNK:(c + 1) * FF_CHUNK])
        h = jnp.square(jnp.maximum(h, 0.0)).astype(BF16)
        acc = acc + _dot(h, w2_ref[c * FF_CHUNK:(c + 1) * FF_CHUNK, :])
    if final_norm:
        acc = _rms(acc, gf_ref[...])
    if batch_major_out:
        ot_buf, = maybe_ot_buf
        t_steps = x.shape[0] // NB
        for c in range(D_MODEL // LANES):
            ot_buf[c] = acc[:, c * LANES:(c + 1) * LANES]
        for b in range(NB):
            for c in range(D_MODEL // LANES):
                o_ref[b, :, c * LANES:(c + 1) * LANES] = ot_buf[c, pl.ds(b, t_steps, stride=NB), :]
    else:
        o_ref[...] = acc


def _mlp(x, g, w1, w2, gf, layer, *, tm, final_norm, batch_major_out=False):
    n = x.shape[0]
    x_spec = pl.BlockSpec((tm, D_MODEL), lambda i: (i, 0))
    if batch_major_out:
        t_steps = tm // NB
        out_spec = pl.BlockSpec((NB, t_steps, D_MODEL), lambda i: (0, i, 0))
        out_shape = jax.ShapeDtypeStruct((NB, n // NB, D_MODEL), F32)
        scratch = [pltpu.VMEM((D_MODEL // LANES, tm, LANES), F32)]
    else:
        out_spec, out_shape, scratch = x_spec, jax.ShapeDtypeStruct(x.shape, F32), []
    return pl.pallas_call(
        functools.partial(_mlp_kernel, final_norm=final_norm, batch_major_out=batch_major_out),
        grid=(n // tm,),
        in_specs=[x_spec, _layer_spec(g, layer), _layer_spec(w1, layer), _layer_spec(w2, layer),
                  _const_spec(gf.shape)],
        out_specs=out_spec,
        out_shape=out_shape,
        scratch_shapes=scratch,
        compiler_params=_params(1),
        name="mlp",
    )(x, g, w1, w2, gf)


def _gate_blocks(wa, wx):
    n_e = wa.shape[0]
    hpb = MXU_K // DH_B
    eye = jnp.eye(hpb, dtype=wa.dtype)

    def bd(w):
        w = w.reshape(n_e, H_B // hpb, hpb, DH_B, DH_B)
        return jnp.einsum('eghij,hk->eghikj', w, eye).reshape(n_e, H_B // hpb, MXU_K, MXU_K)

    return jnp.concatenate([bd(wa), bd(wx)], axis=-1)


def _rows(v):
    return v.reshape(v.shape[0], 1, v.shape[1])


def kernel(x_prompt, x_sample, state_conv, state_rglru, state_pool, norm_mix, norm_ffn, norm_final, w_in, w_out, v_norm, sgu_w, sgu_b, conv_w, conv_b, gate_a_w, gate_a_b, gate_x_w, gate_x_b, lru_lambda, pool_w, pool_b, pool_scale, ffn_w1, ffn_w2):
    n_p, seq, _ = x_prompt.shape
    n_s = x_sample.shape[0]
    depth = norm_mix.shape[0]
    assert n_p == NB and seq % CHUNK == 0

    tril = jnp.tril(jnp.ones((CHUNK, CHUNK), dtype=bool))
    wm = jnp.where(tril, sgu_w, 0.0)
    sw = jnp.concatenate([wm[:, 0::2], wm[:, 1::2]], axis=3).astype(BF16)
    sb = jnp.repeat(jnp.swapaxes(sgu_b, 1, 2), DH_A, axis=2)
    wg = _gate_blocks(gate_a_w, gate_x_w).astype(BF16)
    bg = _rows(jnp.concatenate([gate_a_b, gate_x_b], axis=1))
    w00 = _rows(jnp.repeat(wm[:, :, 0, 0], DH_A, axis=1))
    b0 = _rows(jnp.repeat(sgu_b[:, :, 0], DH_A, axis=1))
    win, wout = w_in.astype(BF16), w_out.astype(BF16)
    w1, w2 = ffn_w1.astype(BF16), ffn_w2.astype(BF16)
    wp = pool_w.astype(BF16)
    g_mix_even, g_mix_odd = _rows(norm_mix[0::2]), _rows(norm_mix[1::2])
    g_ffn = _rows(norm_ffn)
    g_fin = norm_final.reshape(1, D_MODEL)
    even_tail = (conv_w, _rows(conv_b), wg, bg, _rows(lru_lambda))
    even_p = (g_mix_even, win, wout, _rows(v_norm), sw, sb) + even_tail
    even_s = (g_mix_even, win, wout, _rows(v_norm), w00, b0) + even_tail
    pool_params = (g_mix_odd, wp, _rows(pool_b), _rows(pool_scale))

    conv_state = jnp.swapaxes(state_conv, 1, 2)
    pool_state = jnp.swapaxes(state_pool, 1, 2)

    xp = x_prompt
    xs = x_sample.reshape(n_s, D_MODEL)
    sgu_v_s, conv_p, conv_s, h_p, h_s, pool_p, pool_s = [], [], [], [], [], [], []
    for layer in range(depth):
        if layer % 2 == 0:
            e = layer // 2
            xp, cp, hp = _even_prompt(xp, even_p, e, batch_major_in=(layer == 0))
            xs, vs, cs, hs = _even_sample(xs, conv_state, state_rglru, even_s, e)
            sgu_v_s.append(vs.reshape(n_s, 1, W_A))
            conv_p.append(cp)
            conv_s.append(cs)
            h_p.append(hp)
            h_s.append(hs)
        else:
            o = layer // 2
            xp, pp = _pool_prompt(xp, pool_params, o)
            xs, ps = _pool_sample(xs, pool_state, pool_params, o)
            pool_p.append(pp)
            pool_s.append(ps)
        last = layer == depth - 1
        xp = _mlp(xp, g_ffn, w1, w2, g_fin, layer, tm=MLP_TM, final_norm=last,
                  batch_major_out=last)
        xs = _mlp(xs, g_ffn, w1, w2, g_fin, layer, tm=n_s, final_norm=last)

    def stack_positions(parts):
        return jnp.swapaxes(jnp.stack(parts), 1, 2)

    return (xp, xs.reshape(n_s, 1, D_MODEL), jnp.stack(sgu_v_s), stack_positions(conv_p),
            stack_positions(conv_s), jnp.stack(h_p), jnp.stack(h_s), stack_positions(pool_p),
            stack_positions(pool_s))
```

```python
import functools
import math

import jax
import jax.numpy as jnp
from jax import lax
from jax.experimental import pallas as pl
from jax.experimental.pallas import tpu as pltpu

D_MODEL = 1024
CHUNK = 128
W_A = D_MODEL // 2
H_A = 8
DH_A = W_A // H_A
W_B = D_MODEL // 2
H_B = 8
DH_B = W_B // H_B
CONV_W = 4
LRU_C = 8.0
POOL_WINDOWS = (2, 4, 8, 16)
N_POOL = len(POOL_WINDOWS)
G_POOL = D_MODEL // N_POOL
POOL_HIST = max(POOL_WINDOWS) - 1
D_FF = 4 * D_MODEL
EPS = 1e-6
PAST_LEN = 16384

BF16 = jnp.bfloat16
F32 = jnp.float32

VMEM_LIMIT_BYTES = 56 * 1024 * 1024
LANES = 128
NB = 8
MXU_K = 256
EVEN_RB = 512
POOL_T = 128
MLP_TM = 1024
FF_CHUNK = 1024


def _rms(x, g):
    return x * lax.rsqrt(jnp.mean(x * x, axis=-1, keepdims=True) + EPS) * g


def _dot(a, b):
    return jnp.dot(a, b, preferred_element_type=F32)


_GELU_K1 = -2.0 * math.sqrt(2.0 / math.pi) * math.log2(math.e)
_GELU_K2 = _GELU_K1 * 0.044715


def _gelu(x):
    return x / (1.0 + jnp.exp2(x * (_GELU_K1 + _GELU_K2 * (x * x))))


def _sigmoid(x):
    return 0.5 * jnp.tanh(0.5 * x) + 0.5


def _softplus(z):
    return jnp.maximum(z, 0.0) + jnp.log1p(jnp.exp(-jnp.abs(z)))


def _const_spec(shape):
    nd = len(shape)
    return pl.BlockSpec(shape, lambda *_: (0,) * nd)


def _layer_spec(arr, layer):
    nd = arr.ndim - 1
    return pl.BlockSpec((None,) + arr.shape[1:], lambda *_: (layer,) + (0,) * nd)


def _params(n_axes):
    return pltpu.CompilerParams(
        dimension_semantics=("arbitrary",) * n_axes,
        vmem_limit_bytes=VMEM_LIMIT_BYTES)


def _lru_coeffs(xc, wg_ref, bg_ref, lam_ref):
    xcb = xc.astype(BF16)
    rs, is_ = [], []
    for kb in range(W_B // MXU_K):
        gk = _dot(xcb[:, kb * MXU_K:(kb + 1) * MXU_K], wg_ref[kb])
        rs.append(gk[:, :MXU_K])
        is_.append(gk[:, MXU_K:])
    bg = bg_ref[...]
    r = _sigmoid(jnp.concatenate(rs, axis=1) + bg[:, :W_B])
    i = _sigmoid(jnp.concatenate(is_, axis=1) + bg[:, W_B:])
    log_a = r * (-LRU_C * _softplus(-lam_ref[...]))
    a = jnp.exp(log_a)
    t = jnp.tanh(log_a)
    n = -2.0 * t
    mult = jnp.where(n > 0.0, n * lax.rsqrt(n * (1.0 - t)), 0.0)
    return a, mult * (i * xc)


def _even_prompt_kernel(*refs, batch_major_in, cast_mlp):
    (x_ref, g_ref, win_ref, wout_ref, vn_ref, sw_ref, sb_ref,
     cw_ref, cb_ref, wg_ref, bg_ref, lam_ref) = refs[:12]
    refs = refs[12:]
    if cast_mlp:
        w1n_ref, w2n_ref = refs[:2]
        xo_ref, conv_ref, h_ref, w1o_ref, w2o_ref = refs[2:7]
        refs = refs[7:]
        w1o_ref[...] = w1n_ref[...].astype(BF16)
        w2o_ref[...] = w2n_ref[...].astype(BF16)
    else:
        xo_ref, conv_ref, h_ref = refs[:3]
        refs = refs[3:]
    xb_buf, v_buf, s_buf, xn_buf, u_buf, h_carry, *maybe_xt_buf = refs
    rows = NB * CHUNK
    hist_rows = (CONV_W - 1) * NB
    n_rb = rows // EVEN_RB
    t_rb = EVEN_RB // NB
    n_xcol = D_MODEL // LANES
    j = pl.program_id(0)

    @pl.when(j == 0)
    def _():
        xb_buf[0:hist_rows, :] = jnp.zeros((hist_rows, W_B), F32)
        h_carry[...] = jnp.zeros_like(h_carry)

    def load_x(r0):
        if batch_major_in:
            xt_buf, = maybe_xt_buf
            return jnp.concatenate([xt_buf[c, r0:r0 + EVEN_RB, :] for c in range(n_xcol)], axis=1)
        return x_ref[r0:r0 + EVEN_RB, :]

    for k in range(n_rb):
        r0 = k * EVEN_RB
        if batch_major_in:
            xt_buf, = maybe_xt_buf
            for b in range(NB):
                for c in range(n_xcol):
                    xt_buf[c, pl.ds(r0 + b, t_rb, stride=NB), :] = (
                        x_ref[b, k * t_rb:(k + 1) * t_rb, c * LANES:(c + 1) * LANES])
        xn = _rms(load_x(r0), g_ref[...]).astype(BF16)
        xn_buf[r0:r0 + EVEN_RB, :] = xn
        proj_a = _dot(xn, win_ref[:, 0:2 * W_A])
        u_buf[r0:r0 + EVEN_RB, :] = _gelu(proj_a[:, :W_A])
        v = _rms(_gelu(proj_a[:, W_A:]), vn_ref[...])
        for p in range(W_A // LANES):
            v_buf[p, r0:r0 + EVEN_RB, :] = v[:, p * LANES:(p + 1) * LANES]

    lane = lax.broadcasted_iota(jnp.int32, (CHUNK, LANES), 1)
    zero = jnp.zeros((CHUNK, LANES), BF16)
    for p in range(W_A // LANES):
        los, his = [], []
        for b in range(NB):
            vb = v_buf[p, pl.ds(b, CHUNK, stride=NB), :].astype(BF16)
            los.append(jnp.where(lane < DH_A, vb, zero))
            his.append(jnp.where(lane >= DH_A, vb, zero))
        rhs = jnp.concatenate([jnp.concatenate(los, axis=1),
                               jnp.concatenate(his, axis=1)], axis=0)
        o = _dot(sw_ref[p], rhs)
        sbp = sb_ref[:, p * LANES:(p + 1) * LANES]
        for b in range(NB):
            s_buf[p, pl.ds(b, CHUNK, stride=NB), :] = o[:, b * LANES:(b + 1) * LANES] + sbp

    cw = cw_ref[...]
    h = h_carry[...]
    def proj_b_block(k):
        return _dot(xn_buf[k * EVEN_RB:(k + 1) * EVEN_RB, :], win_ref[:, 2 * W_A:])

    proj_b_next = proj_b_block(0)
    for k in range(n_rb):
        r0 = k * EVEN_RB
        proj_b = proj_b_next
        if k + 1 < n_rb:
            proj_b_next = proj_b_block(k + 1)
        gate_act = _gelu(proj_b[:, :W_B])
        xb_buf[hist_rows + r0:hist_rows + r0 + EVEN_RB, :] = proj_b[:, W_B:]
        xc = cb_ref[...] + sum(xb_buf[r0 + kk * NB:r0 + kk * NB + EVEN_RB, :] * cw[kk:kk + 1, :]
                               for kk in range(CONV_W))
        a, bt = _lru_coeffs(xc, wg_ref, bg_ref, lam_ref)
        hs = []
        for t in range(t_rb):
            h = a[t * NB:(t + 1) * NB, :] * h + bt[t * NB:(t + 1) * NB, :]
            hs.append(h)
        b_out = (jnp.concatenate(hs, axis=0) * gate_act).astype(BF16)
        s = jnp.concatenate([s_buf[p, r0:r0 + EVEN_RB, :] for p in range(W_A // LANES)], axis=1)
        a_out = (u_buf[r0:r0 + EVEN_RB, :] * s).astype(BF16)
        y = _dot(jnp.concatenate([a_out, b_out], axis=1), wout_ref[...])
        xo_ref[r0:r0 + EVEN_RB, :] = load_x(r0) + y
    h_carry[...] = h
    tail = xb_buf[rows:rows + hist_rows, :]
    xb_buf[0:hist_rows, :] = tail

    @pl.when(j == pl.num_programs(0) - 1)
    def _():
        conv_ref[...] = tail.reshape(CONV_W - 1, NB, W_B)
        h_ref[...] = h


def _even_prompt(x, layer_params, e, *, batch_major_in, cast_mlp=None):
    seq = x.shape[1] if batch_major_in else x.shape[0] // NB
    rows = NB * CHUNK
    if batch_major_in:
        x_spec = pl.BlockSpec((NB, CHUNK, D_MODEL), lambda j: (0, j, 0))
    else:
        x_spec = pl.BlockSpec((rows, D_MODEL), lambda j: (j, 0))
    slab = lambda width: pltpu.VMEM((width // LANES, rows, LANES), F32)
    scratch = [pltpu.VMEM(((CONV_W - 1) * NB + rows, W_B), F32),
               slab(W_A), slab(W_A),
               pltpu.VMEM((rows, D_MODEL), BF16), pltpu.VMEM((rows, W_A), F32),
               pltpu.VMEM((NB, W_B), F32)]
    if batch_major_in:
        scratch.append(slab(D_MODEL))
    n_steps = seq // CHUNK
    in_specs = [x_spec] + [_layer_spec(p, e) for p in layer_params]
    out_specs = [pl.BlockSpec((rows, D_MODEL), lambda j: (j, 0)),
                 _const_spec((CONV_W - 1, NB, W_B)),
                 _const_spec((NB, W_B))]
    out_shape = [jax.ShapeDtypeStruct((seq * NB, D_MODEL), F32),
                 jax.ShapeDtypeStruct((CONV_W - 1, NB, W_B), F32),
                 jax.ShapeDtypeStruct((NB, W_B), F32)]
    args = [x, *layer_params]
    if cast_mlp:
        c_in, c_out, c_shape = _cast_specs(*cast_mlp, n_steps)
        in_specs += c_in
        out_specs += c_out
        out_shape += c_shape
        args += list(cast_mlp[:2])
    return pl.pallas_call(
        functools.partial(_even_prompt_kernel, batch_major_in=batch_major_in,
                          cast_mlp=bool(cast_mlp)),
        grid=(n_steps,),
        in_specs=in_specs,
        out_specs=out_specs,
        out_shape=out_shape,
        scratch_shapes=scratch,
        compiler_params=_params(1),
        name="even_prompt",
    )(*args)


def _even_sample_kernel(x_ref, conv_ref, h0_ref, g_ref, win_ref, wout_ref, vn_ref,
                        w00_ref, b0_ref, cw_ref, cb_ref, wg_ref, bg_ref, lam_ref,
                        xo_ref, v_ref, convo_ref, ho_ref):
    x = x_ref[...]
    xn = _rms(x, g_ref[...]).astype(BF16)
    proj = _dot(xn, win_ref[...])
    u = _gelu(proj[:, :W_A])
    v = _rms(_gelu(proj[:, W_A:2 * W_A]), vn_ref[...])
    gate = proj[:, 2 * W_A:2 * W_A + W_B]
    xb = proj[:, 2 * W_A + W_B:]
    v_ref[...] = v
    a_out = (u * (v * w00_ref[...] + b0_ref[...])).astype(BF16)

    cw = cw_ref[...]
    xc = cb_ref[...] + xb * cw[CONV_W - 1:CONV_W, :]
    for k in range(CONV_W - 1):
        hk = conv_ref[k]
        xc = xc + hk * cw[k:k + 1, :]
        if k > 0:
            convo_ref[k - 1] = hk
    convo_ref[CONV_W - 2] = xb

    a, bt = _lru_coeffs(xc, wg_ref, bg_ref, lam_ref)
    h = a * h0_ref[...] + bt
    ho_ref[...] = h
    b_out = (h * _gelu(gate)).astype(BF16)
    y = _dot(jnp.concatenate([a_out, b_out], axis=1), wout_ref[...])
    xo_ref[...] = x + y


def _even_sample(x, state_conv, state_h, layer_params, e):
    n = x.shape[0]
    return pl.pallas_call(
        _even_sample_kernel,
        grid=(1,),
        in_specs=[_const_spec(x.shape), _layer_spec(state_conv, e), _layer_spec(state_h, e)]
                 + [_layer_spec(p, e) for p in layer_params],
        out_specs=[_const_spec((n, D_MODEL)), _const_spec((n, W_A)),
                   _const_spec(state_conv.shape[1:]), _const_spec((n, W_B))],
        out_shape=[jax.ShapeDtypeStruct((n, D_MODEL), F32),
                   jax.ShapeDtypeStruct((n, W_A), F32),
                   jax.ShapeDtypeStruct(state_conv.shape[1:], F32),
                   jax.ShapeDtypeStruct((n, W_B), F32)],
        compiler_params=_params(1),
        name="even_sample",
    )(x, state_conv, state_h, *layer_params)


def _pool_prompt_kernel(x_ref, g_ref, wp_ref, bp_ref, sc_ref, xo_ref, hist_ref, z_buf):
    rows = x_ref.shape[0]
    t_steps = rows // NB
    pad = (POOL_HIST + 1) * NB
    j = pl.program_id(0)

    @pl.when(j == 0)
    def _():
        z_buf[0:pad, :] = jnp.zeros((pad, D_MODEL), F32)

    x = x_ref[...]
    xn = _rms(x, g_ref[...])
    z_buf[pad:pad + rows, :] = xn
    row = lax.broadcasted_iota(jnp.int32, (rows, 1), 0)
    pos = j * t_steps + jnp.right_shift(row, NB.bit_length() - 1)
    ys = []
    for gi, w in enumerate(POOL_WINDOWS):
        c0 = gi * G_POOL
        acc = z_buf[pad - (w - 1) * NB:pad + rows, c0:c0 + G_POOL]
        span = 1
        while span < w:
            acc = acc[span * NB:, :] + acc[:-span * NB, :]
            span *= 2
        cnt = jnp.minimum(pos + 1, w).astype(F32)
        p = (acc / cnt - xn[:, c0:c0 + G_POOL]).astype(BF16)
        ys.append(_dot(p, wp_ref[gi]))
    y = (jnp.concatenate(ys, axis=1) + bp_ref[...]) * sc_ref[...]
    xo_ref[...] = x + y
    tail = z_buf[rows + NB:rows + pad, :]
    z_buf[NB:pad, :] = tail

    @pl.when(j == pl.num_programs(0) - 1)
    def _():
        hist_ref[...] = tail.reshape(POOL_HIST, NB, D_MODEL)


def _pool_prompt(x, layer_params, o):
    n_rows = x.shape[0]
    rows = POOL_T * NB
    x_spec = pl.BlockSpec((rows, D_MODEL), lambda j: (j, 0))
    return pl.pallas_call(
        _pool_prompt_kernel,
        grid=(n_rows // rows,),
        in_specs=[x_spec] + [_layer_spec(p, o) for p in layer_params],
        out_specs=[x_spec, _const_spec((POOL_HIST, NB, D_MODEL))],
        out_shape=[jax.ShapeDtypeStruct(x.shape, F32),
                   jax.ShapeDtypeStruct((POOL_HIST, NB, D_MODEL), F32)],
        scratch_shapes=[pltpu.VMEM(((POOL_HIST + 1) * NB + rows, D_MODEL), F32)],
        compiler_params=_params(1),
        name="pool_prompt",
    )(x, *layer_params)


def _pool_sample_kernel(x_ref, hist_ref, g_ref, wp_ref, bp_ref, sc_ref, xo_ref, histo_ref):
    x = x_ref[...]
    xn = _rms(x, g_ref[...])
    ys = []
    for gi, w in enumerate(POOL_WINDOWS):
        c0 = gi * G_POOL
        xg = xn[:, c0:c0 + G_POOL]
        wsum = xg
        for d in range(1, w):
            wsum = wsum + hist_ref[POOL_HIST - d, :, c0:c0 + G_POOL]
        cnt = float(min(PAST_LEN + 1, w))
        p = (wsum / cnt - xg).astype(BF16)
        ys.append(_dot(p, wp_ref[gi]))
    y = (jnp.concatenate(ys, axis=1) + bp_ref[...]) * sc_ref[...]
    xo_ref[...] = x + y
    histo_ref[0:POOL_HIST - 1] = hist_ref[1:POOL_HIST]
    histo_ref[POOL_HIST - 1] = xn


def _pool_sample(x, state_pool, layer_params, o):
    n = x.shape[0]
    return pl.pallas_call(
        _pool_sample_kernel,
        grid=(1,),
        in_specs=[_const_spec(x.shape), _layer_spec(state_pool, o)]
                 + [_layer_spec(p, o) for p in layer_params],
        out_specs=[_const_spec((n, D_MODEL)), _const_spec(state_pool.shape[1:])],
        out_shape=[jax.ShapeDtypeStruct((n, D_MODEL), F32),
                   jax.ShapeDtypeStruct(state_pool.shape[1:], F32)],
        compiler_params=_params(1),
        name="pool_sample",
    )(x, state_pool, *layer_params)


def _cast_specs(w1_f32, w2_f32, layer, n_steps):
    r1, r2 = D_MODEL // n_steps, D_FF // n_steps
    in_specs = [pl.BlockSpec((None, r1, D_FF), lambda i: (layer, i, 0)),
                pl.BlockSpec((None, r2, D_MODEL), lambda i: (layer, i, 0))]
    out_specs = [pl.BlockSpec((r1, D_FF), lambda i: (i, 0)),
                 pl.BlockSpec((r2, D_MODEL), lambda i: (i, 0))]
    out_shape = [jax.ShapeDtypeStruct((D_MODEL, D_FF), BF16),
                 jax.ShapeDtypeStruct((D_FF, D_MODEL), BF16)]
    return in_specs, out_specs, out_shape


def _mlp_kernel(*refs, final_norm, batch_major_out, cast_next):
    x_ref, g_ref, w1_ref, w2_ref, gf_ref = refs[:5]
    refs = refs[5:]
    if cast_next:
        w1n_ref, w2n_ref, o_ref, w1o_ref, w2o_ref = refs[:5]
        refs = refs[5:]
        w1o_ref[...] = w1n_ref[...].astype(BF16)
        w2o_ref[...] = w2n_ref[...].astype(BF16)
    else:
        o_ref = refs[0]
        refs = refs[1:]
    x = x_ref[...]
    xn = _rms(x, g_ref[...]).astype(BF16)
    acc = x
    for c in range(D_FF // FF_CHUNK):
        h = _dot(xn, w1_ref[:, c * FF_CHUNK:(c + 1) * FF_CHUNK])
        h = jnp.square(jnp.maximum(h, 0.0)).astype(BF16)
        acc = acc + _dot(h, w2_ref[c * FF_CHUNK:(c + 1) * FF_CHUNK, :])
    if final_norm:
        acc = _rms(acc, gf_ref[...])
    if batch_major_out:
        ot_buf, = refs
        t_steps = x.shape[0] // NB
        for c in range(D_MODEL // LANES):
            ot_buf[c] = acc[:, c * LANES:(c + 1) * LANES]
        for b in range(NB):
            for c in range(D_MODEL // LANES):
                o_ref[b, :, c * LANES:(c + 1) * LANES] = ot_buf[c, pl.ds(b, t_steps, stride=NB), :]
    else:
        o_ref[...] = acc


def _mlp(x, g, w1, w2, gf, layer, *, tm, final_norm, batch_major_out=False, cast_next=None):
    n = x.shape[0]
    n_steps = n // tm
    x_spec = pl.BlockSpec((tm, D_MODEL), lambda i: (i, 0))
    in_specs = [x_spec, _layer_spec(g, layer), _const_spec(w1.shape), _const_spec(w2.shape),
                _const_spec(gf.shape)]
    args = [x, g, w1, w2, gf]
    if batch_major_out:
        t_steps = tm // NB
        out_specs = [pl.BlockSpec((NB, t_steps, D_MODEL), lambda i: (0, i, 0))]
        out_shape = [jax.ShapeDtypeStruct((NB, n // NB, D_MODEL), F32)]
        scratch = [pltpu.VMEM((D_MODEL // LANES, tm, LANES), F32)]
    else:
        out_specs, out_shape, scratch = [x_spec], [jax.ShapeDtypeStruct(x.shape, F32)], []
    if cast_next:
        c_in, c_out, c_shape = _cast_specs(*cast_next, layer + 1, n_steps)
        in_specs += c_in
        out_specs += c_out
        out_shape += c_shape
        args += list(cast_next)
    outs = pl.pallas_call(
        functools.partial(_mlp_kernel, final_norm=final_norm, batch_major_out=batch_major_out,
                          cast_next=bool(cast_next)),
        grid=(n_steps,),
        in_specs=in_specs,
        out_specs=out_specs,
        out_shape=out_shape,
        scratch_shapes=scratch,
        compiler_params=_params(1),
        name="mlp",
    )(*args)
    return outs if cast_next else outs[0]


def _gate_blocks(wa, wx):
    n_e = wa.shape[0]
    hpb = MXU_K // DH_B
    eye = jnp.eye(hpb, dtype=wa.dtype)

    def bd(w):
        w = w.reshape(n_e, H_B // hpb, hpb, DH_B, DH_B)
        return jnp.einsum('eghij,hk->eghikj', w, eye).reshape(n_e, H_B // hpb, MXU_K, MXU_K)

    return jnp.concatenate([bd(wa), bd(wx)], axis=-1)


def _rows(v):
    return v.reshape(v.shape[0], 1, v.shape[1])


def kernel(x_prompt, x_sample, state_conv, state_rglru, state_pool, norm_mix, norm_ffn, norm_final, w_in, w_out, v_norm, sgu_w, sgu_b, conv_w, conv_b, gate_a_w, gate_a_b, gate_x_w, gate_x_b, lru_lambda, pool_w, pool_b, pool_scale, ffn_w1, ffn_w2):
    n_p, seq, _ = x_prompt.shape
    n_s = x_sample.shape[0]
    depth = norm_mix.shape[0]
    assert n_p == NB and seq % CHUNK == 0

    tril = jnp.tril(jnp.ones((CHUNK, CHUNK), dtype=bool))
    wm = jnp.where(tril, sgu_w, 0.0)
    sw = jnp.concatenate([wm[:, 0::2], wm[:, 1::2]], axis=3).astype(BF16)
    sb = jnp.repeat(jnp.swapaxes(sgu_b, 1, 2), DH_A, axis=2)
    wg = _gate_blocks(gate_a_w, gate_x_w).astype(BF16)
    bg = _rows(jnp.concatenate([gate_a_b, gate_x_b], axis=1))
    w00 = _rows(jnp.repeat(wm[:, :, 0, 0], DH_A, axis=1))
    b0 = _rows(jnp.repeat(sgu_b[:, :, 0], DH_A, axis=1))
    win, wout = w_in.astype(BF16), w_out.astype(BF16)
    wp = pool_w.astype(BF16)
    g_mix_even, g_mix_odd = _rows(norm_mix[0::2]), _rows(norm_mix[1::2])
    g_ffn = _rows(norm_ffn)
    g_fin = norm_final.reshape(1, D_MODEL)
    even_tail = (conv_w, _rows(conv_b), wg, bg, _rows(lru_lambda))
    even_p = (g_mix_even, win, wout, _rows(v_norm), sw, sb) + even_tail
    even_s = (g_mix_even, win, wout, _rows(v_norm), w00, b0) + even_tail
    pool_params = (g_mix_odd, wp, _rows(pool_b), _rows(pool_scale))

    conv_state = jnp.swapaxes(state_conv, 1, 2)
    pool_state = jnp.swapaxes(state_pool, 1, 2)

    xp = x_prompt
    xs = x_sample.reshape(n_s, D_MODEL)
    sgu_v_s, conv_p, conv_s, h_p, h_s, pool_p, pool_s = [], [], [], [], [], [], []
    for layer in range(depth):
        if layer % 2 == 0:
            e = layer // 2
            if layer == 0:
                xp, cp, hp, w1, w2 = _even_prompt(xp, even_p, e, batch_major_in=True,
                                                  cast_mlp=(ffn_w1, ffn_w2, 0))
            else:
                xp, cp, hp = _even_prompt(xp, even_p, e, batch_major_in=False)
            xs, vs, cs, hs = _even_sample(xs, conv_state, state_rglru, even_s, e)
            sgu_v_s.append(vs.reshape(n_s, 1, W_A))
            conv_p.append(cp)
            conv_s.append(cs)
            h_p.append(hp)
            h_s.append(hs)
        else:
            o = layer // 2
            xp, pp = _pool_prompt(xp, pool_params, o)
            xs, ps = _pool_sample(xs, pool_state, pool_params, o)
            pool_p.append(pp)
            pool_s.append(ps)
        last = layer == depth - 1
        xs = _mlp(xs, g_ffn, w1, w2, g_fin, layer, tm=n_s, final_norm=last)
        if last:
            xp = _mlp(xp, g_ffn, w1, w2, g_fin, layer, tm=MLP_TM, final_norm=True,
                      batch_major_out=True)
        else:
            xp, w1, w2 = _mlp(xp, g_ffn, w1, w2, g_fin, layer, tm=MLP_TM, final_norm=False,
                              cast_next=(ffn_w1, ffn_w2))

    def stack_positions(parts):
        return jnp.swapaxes(jnp.stack(parts), 1, 2)

    return (xp, xs.reshape(n_s, 1, D_MODEL), jnp.stack(sgu_v_s), stack_positions(conv_p),
            stack_positions(conv_s), jnp.stack(h_p), jnp.stack(h_s), stack_positions(pool_p),
            stack_positions(pool_s))
```

```python
import functools
import math

import jax
import jax.numpy as jnp
from jax import lax
from jax.experimental import pallas as pl
from jax.experimental.pallas import tpu as pltpu

D_MODEL = 1024
CHUNK = 128
W_A = D_MODEL // 2
H_A = 8
DH_A = W_A // H_A
W_B = D_MODEL // 2
H_B = 8
DH_B = W_B // H_B
CONV_W = 4
LRU_C = 8.0
POOL_WINDOWS = (2, 4, 8, 16)
N_POOL = len(POOL_WINDOWS)
G_POOL = D_MODEL // N_POOL
POOL_HIST = max(POOL_WINDOWS) - 1
D_FF = 4 * D_MODEL
EPS = 1e-6
PAST_LEN = 16384

BF16 = jnp.bfloat16
F32 = jnp.float32

VMEM_LIMIT_BYTES = 56 * 1024 * 1024
LANES = 128
NB = 8
MXU_K = 256
EVEN_RB = 512
POOL_MLP_T = 64
MLP_TM = 1024
FF_CHUNK = 1024


def _rms(x, g):
    return x * lax.rsqrt(jnp.mean(x * x, axis=-1, keepdims=True) + EPS) * g


def _dot(a, b):
    return jnp.dot(a, b, preferred_element_type=F32)


_GELU_K1 = -2.0 * math.sqrt(2.0 / math.pi) * math.log2(math.e)
_GELU_K2 = _GELU_K1 * 0.044715


def _gelu(x):
    return x / (1.0 + jnp.exp2(x * (_GELU_K1 + _GELU_K2 * (x * x))))


def _sigmoid(x):
    return 0.5 * jnp.tanh(0.5 * x) + 0.5


def _softplus(z):
    return jnp.maximum(z, 0.0) + jnp.log1p(jnp.exp(-jnp.abs(z)))


def _const_spec(shape):
    nd = len(shape)
    return pl.BlockSpec(shape, lambda *_: (0,) * nd)


def _layer_spec(arr, layer):
    nd = arr.ndim - 1
    return pl.BlockSpec((None,) + arr.shape[1:], lambda *_: (layer,) + (0,) * nd)


def _params(n_axes):
    return pltpu.CompilerParams(
        dimension_semantics=("arbitrary",) * n_axes,
        vmem_limit_bytes=VMEM_LIMIT_BYTES)


def _lru_coeffs(xc, wg_ref, bg_ref, lam_ref):
    xcb = xc.astype(BF16)
    rs, is_ = [], []
    for kb in range(W_B // MXU_K):
        gk = _dot(xcb[:, kb * MXU_K:(kb + 1) * MXU_K], wg_ref[kb])
        rs.append(gk[:, :MXU_K])
        is_.append(gk[:, MXU_K:])
    bg = bg_ref[...]
    r = _sigmoid(jnp.concatenate(rs, axis=1) + bg[:, :W_B])
    i = _sigmoid(jnp.concatenate(is_, axis=1) + bg[:, W_B:])
    log_a = r * (-LRU_C * _softplus(-lam_ref[...]))
    a = jnp.exp(log_a)
    t = jnp.tanh(log_a)
    n = -2.0 * t
    mult = jnp.where(n > 0.0, n * lax.rsqrt(n * (1.0 - t)), 0.0)
    return a, mult * (i * xc)


def _even_prompt_kernel(*refs, batch_major_in, cast_mlp):
    (x_ref, g_ref, win_ref, wout_ref, vn_ref, sw_ref, sb_ref,
     cw_ref, cb_ref, wg_ref, bg_ref, lam_ref) = refs[:12]
    refs = refs[12:]
    if cast_mlp:
        w1n_ref, w2n_ref = refs[:2]
        xo_ref, conv_ref, h_ref, w1o_ref, w2o_ref = refs[2:7]
        refs = refs[7:]
        w1o_ref[...] = w1n_ref[...].astype(BF16)
        w2o_ref[...] = w2n_ref[...].astype(BF16)
    else:
        xo_ref, conv_ref, h_ref = refs[:3]
        refs = refs[3:]
    xb_buf, v_buf, s_buf, xn_buf, u_buf, h_carry, *maybe_xt_buf = refs
    rows = NB * CHUNK
    hist_rows = (CONV_W - 1) * NB
    n_rb = rows // EVEN_RB
    t_rb = EVEN_RB // NB
    n_xcol = D_MODEL // LANES
    j = pl.program_id(0)

    @pl.when(j == 0)
    def _():
        xb_buf[0:hist_rows, :] = jnp.zeros((hist_rows, W_B), F32)
        h_carry[...] = jnp.zeros_like(h_carry)

    def load_x(r0):
        if batch_major_in:
            xt_buf, = maybe_xt_buf
            return jnp.concatenate([xt_buf[c, r0:r0 + EVEN_RB, :] for c in range(n_xcol)], axis=1)
        return x_ref[r0:r0 + EVEN_RB, :]

    for k in range(n_rb):
        r0 = k * EVEN_RB
        if batch_major_in:
            xt_buf, = maybe_xt_buf
            for b in range(NB):
                for c in range(n_xcol):
                    xt_buf[c, pl.ds(r0 + b, t_rb, stride=NB), :] = (
                        x_ref[b, k * t_rb:(k + 1) * t_rb, c * LANES:(c + 1) * LANES])
        xn = _rms(load_x(r0), g_ref[...]).astype(BF16)
        xn_buf[r0:r0 + EVEN_RB, :] = xn
        proj_a = _dot(xn, win_ref[:, 0:2 * W_A])
        u_buf[r0:r0 + EVEN_RB, :] = _gelu(proj_a[:, :W_A])
        v = _rms(_gelu(proj_a[:, W_A:]), vn_ref[...])
        for p in range(W_A // LANES):
            v_buf[p, r0:r0 + EVEN_RB, :] = v[:, p * LANES:(p + 1) * LANES]

    lane = lax.broadcasted_iota(jnp.int32, (CHUNK, LANES), 1)
    zero = jnp.zeros((CHUNK, LANES), BF16)
    for p in range(W_A // LANES):
        los, his = [], []
        for b in range(NB):
            vb = v_buf[p, pl.ds(b, CHUNK, stride=NB), :].astype(BF16)
            los.append(jnp.where(lane < DH_A, vb, zero))
            his.append(jnp.where(lane >= DH_A, vb, zero))
        rhs = jnp.concatenate([jnp.concatenate(los, axis=1),
                               jnp.concatenate(his, axis=1)], axis=0)
        o = _dot(sw_ref[p], rhs)
        sbp = sb_ref[:, p * LANES:(p + 1) * LANES]
        for b in range(NB):
            s_buf[p, pl.ds(b, CHUNK, stride=NB), :] = o[:, b * LANES:(b + 1) * LANES] + sbp

    cw = cw_ref[...]
    h = h_carry[...]
    def proj_b_block(k):
        return _dot(xn_buf[k * EVEN_RB:(k + 1) * EVEN_RB, :], win_ref[:, 2 * W_A:])

    proj_b_next = proj_b_block(0)
    for k in range(n_rb):
        r0 = k * EVEN_RB
        proj_b = proj_b_next
        if k + 1 < n_rb:
            proj_b_next = proj_b_block(k + 1)
        gate_act = _gelu(proj_b[:, :W_B])
        xb_buf[hist_rows + r0:hist_rows + r0 + EVEN_RB, :] = proj_b[:, W_B:]
        xc = cb_ref[...] + sum(xb_buf[r0 + kk * NB:r0 + kk * NB + EVEN_RB, :] * cw[kk:kk + 1, :]
                               for kk in range(CONV_W))
        a, bt = _lru_coeffs(xc, wg_ref, bg_ref, lam_ref)
        hs = []
        for t in range(t_rb):
            h = a[t * NB:(t + 1) * NB, :] * h + bt[t * NB:(t + 1) * NB, :]
            hs.append(h)
        b_out = (jnp.concatenate(hs, axis=0) * gate_act).astype(BF16)
        s = jnp.concatenate([s_buf[p, r0:r0 + EVEN_RB, :] for p in range(W_A // LANES)], axis=1)
        a_out = (u_buf[r0:r0 + EVEN_RB, :] * s).astype(BF16)
        y = _dot(jnp.concatenate([a_out, b_out], axis=1), wout_ref[...])
        xo_ref[r0:r0 + EVEN_RB, :] = load_x(r0) + y
    h_carry[...] = h
    tail = xb_buf[rows:rows + hist_rows, :]
    xb_buf[0:hist_rows, :] = tail

    @pl.when(j == pl.num_programs(0) - 1)
    def _():
        conv_ref[...] = tail.reshape(CONV_W - 1, NB, W_B)
        h_ref[...] = h


def _even_prompt(x, layer_params, e, *, batch_major_in, cast_mlp=None):
    seq = x.shape[1] if batch_major_in else x.shape[0] // NB
    rows = NB * CHUNK
    if batch_major_in:
        x_spec = pl.BlockSpec((NB, CHUNK, D_MODEL), lambda j: (0, j, 0))
    else:
        x_spec = pl.BlockSpec((rows, D_MODEL), lambda j: (j, 0))
    slab = lambda width: pltpu.VMEM((width // LANES, rows, LANES), F32)
    scratch = [pltpu.VMEM(((CONV_W - 1) * NB + rows, W_B), F32),
               slab(W_A), slab(W_A),
               pltpu.VMEM((rows, D_MODEL), BF16), pltpu.VMEM((rows, W_A), F32),
               pltpu.VMEM((NB, W_B), F32)]
    if batch_major_in:
        scratch.append(slab(D_MODEL))
    n_steps = seq // CHUNK
    in_specs = [x_spec] + [_layer_spec(p, e) for p in layer_params]
    out_specs = [pl.BlockSpec((rows, D_MODEL), lambda j: (j, 0)),
                 _const_spec((CONV_W - 1, NB, W_B)),
                 _const_spec((NB, W_B))]
    out_shape = [jax.ShapeDtypeStruct((seq * NB, D_MODEL), F32),
                 jax.ShapeDtypeStruct((CONV_W - 1, NB, W_B), F32),
                 jax.ShapeDtypeStruct((NB, W_B), F32)]
    args = [x, *layer_params]
    if cast_mlp:
        c_in, c_out, c_shape = _cast_specs(*cast_mlp, n_steps)
        in_specs += c_in
        out_specs += c_out
        out_shape += c_shape
        args += list(cast_mlp[:2])
    return pl.pallas_call(
        functools.partial(_even_prompt_kernel, batch_major_in=batch_major_in,
                          cast_mlp=bool(cast_mlp)),
        grid=(n_steps,),
        in_specs=in_specs,
        out_specs=out_specs,
        out_shape=out_shape,
        scratch_shapes=scratch,
        compiler_params=_params(1),
        name="even_prompt",
    )(*args)


def _even_sample_kernel(x_ref, conv_ref, h0_ref, g_ref, win_ref, wout_ref, vn_ref,
                        w00_ref, b0_ref, cw_ref, cb_ref, wg_ref, bg_ref, lam_ref,
                        xo_ref, v_ref, convo_ref, ho_ref):
    x = x_ref[...]
    xn = _rms(x, g_ref[...]).astype(BF16)
    proj = _dot(xn, win_ref[...])
    u = _gelu(proj[:, :W_A])
    v = _rms(_gelu(proj[:, W_A:2 * W_A]), vn_ref[...])
    gate = proj[:, 2 * W_A:2 * W_A + W_B]
    xb = proj[:, 2 * W_A + W_B:]
    v_ref[...] = v
    a_out = (u * (v * w00_ref[...] + b0_ref[...])).astype(BF16)

    cw = cw_ref[...]
    xc = cb_ref[...] + xb * cw[CONV_W - 1:CONV_W, :]
    for k in range(CONV_W - 1):
        hk = conv_ref[k]
        xc = xc + hk * cw[k:k + 1, :]
        if k > 0:
            convo_ref[k - 1] = hk
    convo_ref[CONV_W - 2] = xb

    a, bt = _lru_coeffs(xc, wg_ref, bg_ref, lam_ref)
    h = a * h0_ref[...] + bt
    ho_ref[...] = h
    b_out = (h * _gelu(gate)).astype(BF16)
    y = _dot(jnp.concatenate([a_out, b_out], axis=1), wout_ref[...])
    xo_ref[...] = x + y


def _even_sample(x, state_conv, state_h, layer_params, e):
    n = x.shape[0]
    return pl.pallas_call(
        _even_sample_kernel,
        grid=(1,),
        in_specs=[_const_spec(x.shape), _layer_spec(state_conv, e), _layer_spec(state_h, e)]
                 + [_layer_spec(p, e) for p in layer_params],
        out_specs=[_const_spec((n, D_MODEL)), _const_spec((n, W_A)),
                   _const_spec(state_conv.shape[1:]), _const_spec((n, W_B))],
        out_shape=[jax.ShapeDtypeStruct((n, D_MODEL), F32),
                   jax.ShapeDtypeStruct((n, W_A), F32),
                   jax.ShapeDtypeStruct(state_conv.shape[1:], F32),
                   jax.ShapeDtypeStruct((n, W_B), F32)],
        compiler_params=_params(1),
        name="even_sample",
    )(x, state_conv, state_h, *layer_params)


def _drive(*stage_gens):
    results = [None] * len(stage_gens)
    live = list(range(len(stage_gens)))
    while live:
        for i in list(live):
            try:
                next(stage_gens[i])
            except StopIteration as stop:
                results[i] = stop.value
                live.remove(i)
    return results


def _pool_mix(x, t0, z_buf, g_ref, wp_ref, bp_ref, sc_ref):
    rows = x.shape[0]
    pad = (POOL_HIST + 1) * NB
    xn = _rms(x, g_ref[...])
    z_buf[pad:pad + rows, :] = xn
    row = lax.broadcasted_iota(jnp.int32, (rows, 1), 0)
    pos = t0 + jnp.right_shift(row, NB.bit_length() - 1)
    ys = []
    for gi, w in enumerate(POOL_WINDOWS):
        yield
        c0 = gi * G_POOL
        acc = z_buf[pad - (w - 1) * NB:pad + rows, c0:c0 + G_POOL]
        span = 1
        while span < w:
            acc = acc[span * NB:, :] + acc[:-span * NB, :]
            span *= 2
        cnt = jnp.minimum(pos + 1, w).astype(F32)
        p = (acc / cnt - xn[:, c0:c0 + G_POOL]).astype(BF16)
        ys.append(_dot(p, wp_ref[gi]))
    y = (jnp.concatenate(ys, axis=1) + bp_ref[...]) * sc_ref[...]
    tail = z_buf[rows + NB:rows + pad, :]
    z_buf[NB:pad, :] = tail
    return x + y, tail


def _pool_mlp_kernel(x_ref, gm_ref, wp_ref, bp_ref, sc_ref, gf_ref, w1_ref, w2_ref, gfin_ref,
                     o_ref, hist_ref, z_buf, p_buf, *maybe_ot_buf, final_norm):
    rows = x_ref.shape[0]
    pad = (POOL_HIST + 1) * NB
    j = pl.program_id(0)
    n_tiles = pl.num_programs(0) - 1

    def pool_stages():
        return _pool_mix(x_ref[...], j * (rows // NB), z_buf, gm_ref, wp_ref, bp_ref, sc_ref)

    def mlp_stages():
        return _mlp_rows(p_buf[lax.rem(j + 1, 2)], gf_ref, w1_ref, w2_ref)

    def finish_pool(mixed):
        xm, tail = mixed
        p_buf[lax.rem(j, 2)] = xm
        hist_ref[...] = tail.reshape(POOL_HIST, NB, D_MODEL)

    def finish_mlp(acc):
        if final_norm:
            _store_batch_major(_rms(acc, gfin_ref[...]), o_ref, *maybe_ot_buf)
        else:
            o_ref[...] = acc

    @pl.when(j == 0)
    def _():
        z_buf[0:pad, :] = jnp.zeros((pad, D_MODEL), F32)
        finish_pool(*_drive(pool_stages()))

    @pl.when(jnp.logical_and(j > 0, j < n_tiles))
    def _():
        acc, mixed = _drive(mlp_stages(), pool_stages())
        finish_mlp(acc)
        finish_pool(mixed)

    @pl.when(j == n_tiles)
    def _():
        finish_mlp(*_drive(mlp_stages()))


def _pool_mlp_prompt(x, pool_params, o, g_ffn, w1, w2, g_fin, layer, *, final_norm):
    n_rows = x.shape[0]
    rows = POOL_MLP_T * NB
    n_tiles = n_rows // rows
    x_spec = pl.BlockSpec((rows, D_MODEL), lambda j: (jnp.minimum(j, n_tiles - 1), 0))
    if final_norm:
        out_spec = pl.BlockSpec((NB, POOL_MLP_T, D_MODEL), lambda j: (0, jnp.maximum(j - 1, 0), 0))
        out_shape = jax.ShapeDtypeStruct((NB, n_rows // NB, D_MODEL), F32)
        extra_scratch = [pltpu.VMEM((D_MODEL // LANES, rows, LANES), F32)]
    else:
        out_spec = pl.BlockSpec((rows, D_MODEL), lambda j: (jnp.maximum(j - 1, 0), 0))
        out_shape = jax.ShapeDtypeStruct(x.shape, F32)
        extra_scratch = []
    return pl.pallas_call(
        functools.partial(_pool_mlp_kernel, final_norm=final_norm),
        grid=(n_tiles + 1,),
        in_specs=[x_spec] + [_layer_spec(p, o) for p in pool_params]
                 + [_layer_spec(g_ffn, layer), _const_spec(w1.shape), _const_spec(w2.shape),
                    _const_spec(g_fin.shape)],
        out_specs=[out_spec, _const_spec((POOL_HIST, NB, D_MODEL))],
        out_shape=[out_shape, jax.ShapeDtypeStruct((POOL_HIST, NB, D_MODEL), F32)],
        scratch_shapes=[pltpu.VMEM(((POOL_HIST + 1) * NB + rows, D_MODEL), F32),
                        pltpu.VMEM((2, rows, D_MODEL), F32)] + extra_scratch,
        compiler_params=_params(1),
        name="pool_mlp",
    )(x, *pool_params, g_ffn, w1, w2, g_fin)


def _pool_sample_kernel(x_ref, hist_ref, g_ref, wp_ref, bp_ref, sc_ref, xo_ref, histo_ref):
    x = x_ref[...]
    xn = _rms(x, g_ref[...])
    ys = []
    for gi, w in enumerate(POOL_WINDOWS):
        c0 = gi * G_POOL
        xg = xn[:, c0:c0 + G_POOL]
        wsum = xg
        for d in range(1, w):
            wsum = wsum + hist_ref[POOL_HIST - d, :, c0:c0 + G_POOL]
        cnt = float(min(PAST_LEN + 1, w))
        p = (wsum / cnt - xg).astype(BF16)
        ys.append(_dot(p, wp_ref[gi]))
    y = (jnp.concatenate(ys, axis=1) + bp_ref[...]) * sc_ref[...]
    xo_ref[...] = x + y
    histo_ref[0:POOL_HIST - 1] = hist_ref[1:POOL_HIST]
    histo_ref[POOL_HIST - 1] = xn


def _pool_sample(x, state_pool, layer_params, o):
    n = x.shape[0]
    return pl.pallas_call(
        _pool_sample_kernel,
        grid=(1,),
        in_specs=[_const_spec(x.shape), _layer_spec(state_pool, o)]
                 + [_layer_spec(p, o) for p in layer_params],
        out_specs=[_const_spec((n, D_MODEL)), _const_spec(state_pool.shape[1:])],
        out_shape=[jax.ShapeDtypeStruct((n, D_MODEL), F32),
                   jax.ShapeDtypeStruct(state_pool.shape[1:], F32)],
        compiler_params=_params(1),
        name="pool_sample",
    )(x, state_pool, *layer_params)


def _cast_specs(w1_f32, w2_f32, layer, n_steps):
    r1, r2 = D_MODEL // n_steps, D_FF // n_steps
    in_specs = [pl.BlockSpec((None, r1, D_FF), lambda i: (layer, i, 0)),
                pl.BlockSpec((None, r2, D_MODEL), lambda i: (layer, i, 0))]
    out_specs = [pl.BlockSpec((r1, D_FF), lambda i: (i, 0)),
                 pl.BlockSpec((r2, D_MODEL), lambda i: (i, 0))]
    out_shape = [jax.ShapeDtypeStruct((D_MODEL, D_FF), BF16),
                 jax.ShapeDtypeStruct((D_FF, D_MODEL), BF16)]
    return in_specs, out_specs, out_shape


def _mlp_rows(x, g_ref, w1_ref, w2_ref):
    xn = _rms(x, g_ref[...]).astype(BF16)
    acc = x
    for c in range(D_FF // FF_CHUNK):
        yield
        h = _dot(xn, w1_ref[:, c * FF_CHUNK:(c + 1) * FF_CHUNK])
        h = jnp.square(jnp.maximum(h, 0.0)).astype(BF16)
        acc = acc + _dot(h, w2_ref[c * FF_CHUNK:(c + 1) * FF_CHUNK, :])
    return acc


def _store_batch_major(acc, o_ref, ot_buf):
    t_steps = acc.shape[0] // NB
    for c in range(D_MODEL // LANES):
        ot_buf[c] = acc[:, c * LANES:(c + 1) * LANES]
    for b in range(NB):
        for c in range(D_MODEL // LANES):
            o_ref[b, :, c * LANES:(c + 1) * LANES] = ot_buf[c, pl.ds(b, t_steps, stride=NB), :]


def _mlp_kernel(*refs, final_norm, cast_next):
    x_ref, g_ref, w1_ref, w2_ref, gf_ref = refs[:5]
    if cast_next:
        w1n_ref, w2n_ref, o_ref, w1o_ref, w2o_ref = refs[5:]
        w1o_ref[...] = w1n_ref[...].astype(BF16)
        w2o_ref[...] = w2n_ref[...].astype(BF16)
    else:
        o_ref, = refs[5:]
    acc, = _drive(_mlp_rows(x_ref[...], g_ref, w1_ref, w2_ref))
    if final_norm:
        acc = _rms(acc, gf_ref[...])
    o_ref[...] = acc


def _mlp(x, g, w1, w2, gf, layer, *, tm, final_norm, cast_next=None):
    n = x.shape[0]
    n_steps = n // tm
    x_spec = pl.BlockSpec((tm, D_MODEL), lambda i: (i, 0))
    in_specs = [x_spec, _layer_spec(g, layer), _const_spec(w1.shape), _const_spec(w2.shape),
                _const_spec(gf.shape)]
    args = [x, g, w1, w2, gf]
    out_specs, out_shape = [x_spec], [jax.ShapeDtypeStruct(x.shape, F32)]
    if cast_next:
        c_in, c_out, c_shape = _cast_specs(*cast_next, layer + 1, n_steps)
        in_specs += c_in
        out_specs += c_out
        out_shape += c_shape
        args += list(cast_next)
    outs = pl.pallas_call(
        functools.partial(_mlp_kernel, final_norm=final_norm, cast_next=bool(cast_next)),
        grid=(n_steps,),
        in_specs=in_specs,
        out_specs=out_specs,
        out_shape=out_shape,
        compiler_params=_params(1),
        name="mlp",
    )(*args)
    return outs if cast_next else outs[0]


def _gate_blocks(wa, wx):
    n_e = wa.shape[0]
    hpb = MXU_K // DH_B
    eye = jnp.eye(hpb, dtype=wa.dtype)

    def bd(w):
        w = w.reshape(n_e, H_B // hpb, hpb, DH_B, DH_B)
        return jnp.einsum('eghij,hk->eghikj', w, eye).reshape(n_e, H_B // hpb, MXU_K, MXU_K)

    return jnp.concatenate([bd(wa), bd(wx)], axis=-1)


def _rows(v):
    return v.reshape(v.shape[0], 1, v.shape[1])


def kernel(x_prompt, x_sample, state_conv, state_rglru, state_pool, norm_mix, norm_ffn, norm_final, w_in, w_out, v_norm, sgu_w, sgu_b, conv_w, conv_b, gate_a_w, gate_a_b, gate_x_w, gate_x_b, lru_lambda, pool_w, pool_b, pool_scale, ffn_w1, ffn_w2):
    n_p, seq, _ = x_prompt.shape
    n_s = x_sample.shape[0]
    depth = norm_mix.shape[0]
    assert n_p == NB and seq % CHUNK == 0

    tril = jnp.tril(jnp.ones((CHUNK, CHUNK), dtype=bool))
    wm = jnp.where(tril, sgu_w, 0.0)
    sw = jnp.concatenate([wm[:, 0::2], wm[:, 1::2]], axis=3).astype(BF16)
    sb = jnp.repeat(jnp.swapaxes(sgu_b, 1, 2), DH_A, axis=2)
    wg = _gate_blocks(gate_a_w, gate_x_w).astype(BF16)
    bg = _rows(jnp.concatenate([gate_a_b, gate_x_b], axis=1))
    w00 = _rows(jnp.repeat(wm[:, :, 0, 0], DH_A, axis=1))
    b0 = _rows(jnp.repeat(sgu_b[:, :, 0], DH_A, axis=1))
    win, wout = w_in.astype(BF16), w_out.astype(BF16)
    wp = pool_w.astype(BF16)
    g_mix_even, g_mix_odd = _rows(norm_mix[0::2]), _rows(norm_mix[1::2])
    g_ffn = _rows(norm_ffn)
    g_fin = norm_final.reshape(1, D_MODEL)
    even_tail = (conv_w, _rows(conv_b), wg, bg, _rows(lru_lambda))
    even_p = (g_mix_even, win, wout, _rows(v_norm), sw, sb) + even_tail
    even_s = (g_mix_even, win, wout, _rows(v_norm), w00, b0) + even_tail
    pool_params = (g_mix_odd, wp, _rows(pool_b), _rows(pool_scale))

    conv_state = jnp.swapaxes(state_conv, 1, 2)
    pool_state = jnp.swapaxes(state_pool, 1, 2)

    xp = x_prompt
    xs = x_sample.reshape(n_s, D_MODEL)
    sgu_v_s, conv_p, conv_s, h_p, h_s, pool_p, pool_s = [], [], [], [], [], [], []
    assert depth % 2 == 0
    for layer in range(depth):
        last = layer == depth - 1
        if layer % 2 == 0:
            e = layer // 2
            xp, cp, hp, w1, w2 = _even_prompt(xp, even_p, e, batch_major_in=(layer == 0),
                                              cast_mlp=(ffn_w1, ffn_w2, layer))
            xs, vs, cs, hs = _even_sample(xs, conv_state, state_rglru, even_s, e)
            sgu_v_s.append(vs.reshape(n_s, 1, W_A))
            conv_p.append(cp)
            conv_s.append(cs)
            h_p.append(hp)
            h_s.append(hs)
            xs = _mlp(xs, g_ffn, w1, w2, g_fin, layer, tm=n_s, final_norm=False)
            xp, w1, w2 = _mlp(xp, g_ffn, w1, w2, g_fin, layer, tm=MLP_TM, final_norm=False,
                              cast_next=(ffn_w1, ffn_w2))
        else:
            o = layer // 2
            xs, ps = _pool_sample(xs, pool_state, pool_params, o)
            xs = _mlp(xs, g_ffn, w1, w2, g_fin, layer, tm=n_s, final_norm=last)
            xp, pp = _pool_mlp_prompt(xp, pool_params, o, g_ffn, w1, w2, g_fin, layer,
                                      final_norm=last)
            pool_p.append(pp)
            pool_s.append(ps)

    def stack_positions(parts):
        return jnp.swapaxes(jnp.stack(parts), 1, 2)

    return (xp, xs.reshape(n_s, 1, D_MODEL), jnp.stack(sgu_v_s), stack_positions(conv_p),
            stack_positions(conv_s), jnp.stack(h_p), jnp.stack(h_s), stack_positions(pool_p),
            stack_positions(pool_s))
```

```python
import functools
import math

import jax
import jax.numpy as jnp
from jax import lax
from jax.experimental import pallas as pl
from jax.experimental.pallas import tpu as pltpu

D_MODEL = 1024
CHUNK = 128
W_A = D_MODEL // 2
H_A = 8
DH_A = W_A // H_A
W_B = D_MODEL // 2
H_B = 8
DH_B = W_B // H_B
CONV_W = 4
LRU_C = 8.0
POOL_WINDOWS = (2, 4, 8, 16)
N_POOL = len(POOL_WINDOWS)
G_POOL = D_MODEL // N_POOL
POOL_HIST = max(POOL_WINDOWS) - 1
D_FF = 4 * D_MODEL
EPS = 1e-6
PAST_LEN = 16384

BF16 = jnp.bfloat16
F32 = jnp.float32

VMEM_LIMIT_BYTES = 56 * 1024 * 1024
LANES = 128
NB = 8
MXU_K = 256
EVEN_RB = 512
POOL_MLP_T = 64
MLP_TM = 1024
FF_CHUNK = 1024


def _rms(x, g):
    return x * lax.rsqrt(jnp.mean(x * x, axis=-1, keepdims=True) + EPS) * g


def _dot(a, b):
    return jnp.dot(a, b, preferred_element_type=F32)


_GELU_K1 = -2.0 * math.sqrt(2.0 / math.pi) * math.log2(math.e)
_GELU_K2 = _GELU_K1 * 0.044715


def _gelu(x):
    return x / (1.0 + jnp.exp2(x * (_GELU_K1 + _GELU_K2 * (x * x))))


def _sigmoid(x):
    return 0.5 * jnp.tanh(0.5 * x) + 0.5


def _softplus(z):
    return jnp.maximum(z, 0.0) + jnp.log1p(jnp.exp(-jnp.abs(z)))


def _const_spec(shape):
    nd = len(shape)
    return pl.BlockSpec(shape, lambda *_: (0,) * nd)


def _layer_spec(arr, layer):
    nd = arr.ndim - 1
    return pl.BlockSpec((None,) + arr.shape[1:], lambda *_: (layer,) + (0,) * nd)


def _params(n_axes):
    return pltpu.CompilerParams(
        dimension_semantics=("arbitrary",) * n_axes,
        vmem_limit_bytes=VMEM_LIMIT_BYTES)


def _lru_coeffs(xc, wg_ref, bg_ref, lam_ref):
    xcb = xc.astype(BF16)
    rs, is_ = [], []
    for kb in range(W_B // MXU_K):
        gk = _dot(xcb[:, kb * MXU_K:(kb + 1) * MXU_K], wg_ref[kb])
        rs.append(gk[:, :MXU_K])
        is_.append(gk[:, MXU_K:])
    bg = bg_ref[...]
    r = _sigmoid(jnp.concatenate(rs, axis=1) + bg[:, :W_B])
    i = _sigmoid(jnp.concatenate(is_, axis=1) + bg[:, W_B:])
    log_a = r * (-LRU_C * _softplus(-lam_ref[...]))
    a = jnp.exp(log_a)
    t = jnp.tanh(log_a)
    n = -2.0 * t
    mult = jnp.where(n > 0.0, n * lax.rsqrt(n * (1.0 - t)), 0.0)
    return a, mult * (i * xc)


def _even_prompt_kernel(*refs, batch_major_in, cast_mlp):
    (x_ref, g_ref, win_ref, wout_ref, vn_ref, sw_ref, sb_ref,
     cw_ref, cb_ref, wg_ref, bg_ref, lam_ref) = refs[:12]
    refs = refs[12:]
    if cast_mlp:
        w1n_ref, w2n_ref = refs[:2]
        xo_ref, conv_ref, h_ref, w1o_ref, w2o_ref = refs[2:7]
        refs = refs[7:]
        w1o_ref[...] = w1n_ref[...].astype(BF16)
        w2o_ref[...] = w2n_ref[...].astype(BF16)
    else:
        xo_ref, conv_ref, h_ref = refs[:3]
        refs = refs[3:]
    xb_buf, v_buf, s_buf, u_buf, h_carry, *maybe_xt_buf = refs
    rows = NB * CHUNK
    hist_rows = (CONV_W - 1) * NB
    n_rb = rows // EVEN_RB
    t_rb = EVEN_RB // NB
    n_xcol = D_MODEL // LANES
    j = pl.program_id(0)

    @pl.when(j == 0)
    def _():
        xb_buf[0:hist_rows, :] = jnp.zeros((hist_rows, W_B), F32)
        h_carry[...] = jnp.zeros_like(h_carry)

    def load_x(r0):
        if batch_major_in:
            xt_buf, = maybe_xt_buf
            return jnp.concatenate([xt_buf[c, r0:r0 + EVEN_RB, :] for c in range(n_xcol)], axis=1)
        return x_ref[r0:r0 + EVEN_RB, :]

    val = {}
    cw = cw_ref[...]
    h_cell = [h_carry[...]]
    lane = lax.broadcasted_iota(jnp.int32, (CHUNK, LANES), 1)
    zero = jnp.zeros((CHUNK, LANES), BF16)

    def norm(k):
        r0 = k * EVEN_RB
        if batch_major_in:
            xt_buf, = maybe_xt_buf
            for b in range(NB):
                for c in range(n_xcol):
                    xt_buf[c, pl.ds(r0 + b, t_rb, stride=NB), :] = (
                        x_ref[b, k * t_rb:(k + 1) * t_rb, c * LANES:(c + 1) * LANES])
        val['xn', k] = _rms(load_x(r0), g_ref[...]).astype(BF16)

    def proj_uv(k):
        val['uv', k] = _dot(val['xn', k], win_ref[:, 0:2 * W_A])

    def act_u(k):
        u_buf[k * EVEN_RB:(k + 1) * EVEN_RB, :] = _gelu(val['uv', k][:, :W_A])

    def act_v(k):
        v = _rms(_gelu(val['uv', k][:, W_A:]), vn_ref[...])
        for p in range(W_A // LANES):
            v_buf[p, k * EVEN_RB:(k + 1) * EVEN_RB, :] = v[:, p * LANES:(p + 1) * LANES]

    def proj_gx(k):
        val['gx', k] = _dot(val['xn', k], win_ref[:, 2 * W_A:])

    def gate_conv(k):
        r0 = k * EVEN_RB
        proj_b = val['gx', k]
        val['gate', k] = _gelu(proj_b[:, :W_B])
        xb_buf[hist_rows + r0:hist_rows + r0 + EVEN_RB, :] = proj_b[:, W_B:]
        val['xc', k] = cb_ref[...] + sum(
            xb_buf[r0 + kk * NB:r0 + kk * NB + EVEN_RB, :] * cw[kk:kk + 1, :]
            for kk in range(CONV_W))

    def lru(k):
        a, bt = _lru_coeffs(val['xc', k], wg_ref, bg_ref, lam_ref)
        h, hs = h_cell[0], []
        for t in range(t_rb):
            h = a[t * NB:(t + 1) * NB, :] * h + bt[t * NB:(t + 1) * NB, :]
            hs.append(h)
        h_cell[0] = h
        val['b_out', k] = (jnp.concatenate(hs, axis=0) * val['gate', k]).astype(BF16)

    def spatial_gate(p):
        los, his = [], []
        for b in range(NB):
            vb = v_buf[p, pl.ds(b, CHUNK, stride=NB), :].astype(BF16)
            los.append(jnp.where(lane < DH_A, vb, zero))
            his.append(jnp.where(lane >= DH_A, vb, zero))
        rhs = jnp.concatenate([jnp.concatenate(los, axis=1),
                               jnp.concatenate(his, axis=1)], axis=0)
        o = _dot(sw_ref[p], rhs)
        sbp = sb_ref[:, p * LANES:(p + 1) * LANES]
        for b in range(NB):
            s_buf[p, pl.ds(b, CHUNK, stride=NB), :] = o[:, b * LANES:(b + 1) * LANES] + sbp

    def proj_out(k):
        r0 = k * EVEN_RB
        s = jnp.concatenate([s_buf[p, r0:r0 + EVEN_RB, :] for p in range(W_A // LANES)], axis=1)
        a_out = (u_buf[r0:r0 + EVEN_RB, :] * s).astype(BF16)
        y = _dot(jnp.concatenate([a_out, val['b_out', k]], axis=1), wout_ref[...])
        xo_ref[r0:r0 + EVEN_RB, :] = load_x(r0) + y

    assert n_rb == 2 and W_A // LANES == 4
    for stage, arg in ((norm, 0), (proj_uv, 0), (norm, 1), (proj_gx, 0), (act_u, 0),
                       (proj_uv, 1), (act_v, 0), (gate_conv, 0), (proj_gx, 1), (act_v, 1),
                       (lru, 0), (spatial_gate, 0), (act_u, 1), (spatial_gate, 1),
                       (gate_conv, 1), (spatial_gate, 2), (spatial_gate, 3),
                       (proj_out, 0), (lru, 1), (proj_out, 1)):
        stage(arg)
    h_last = h_cell[0]
    h_carry[...] = h_last
    tail = xb_buf[rows:rows + hist_rows, :]
    xb_buf[0:hist_rows, :] = tail

    @pl.when(j == pl.num_programs(0) - 1)
    def _():
        conv_ref[...] = tail.reshape(CONV_W - 1, NB, W_B)
        h_ref[...] = h_last


def _even_prompt(x, layer_params, e, *, batch_major_in, cast_mlp=None):
    seq = x.shape[1] if batch_major_in else x.shape[0] // NB
    rows = NB * CHUNK
    if batch_major_in:
        x_spec = pl.BlockSpec((NB, CHUNK, D_MODEL), lambda j: (0, j, 0))
    else:
        x_spec = pl.BlockSpec((rows, D_MODEL), lambda j: (j, 0))
    slab = lambda width: pltpu.VMEM((width // LANES, rows, LANES), F32)
    scratch = [pltpu.VMEM(((CONV_W - 1) * NB + rows, W_B), F32),
               slab(W_A), slab(W_A),
               pltpu.VMEM((rows, W_A), F32),
               pltpu.VMEM((NB, W_B), F32)]
    if batch_major_in:
        scratch.append(slab(D_MODEL))
    n_steps = seq // CHUNK
    in_specs = [x_spec] + [_layer_spec(p, e) for p in layer_params]
    out_specs = [pl.BlockSpec((rows, D_MODEL), lambda j: (j, 0)),
                 _const_spec((CONV_W - 1, NB, W_B)),
                 _const_spec((NB, W_B))]
    out_shape = [jax.ShapeDtypeStruct((seq * NB, D_MODEL), F32),
                 jax.ShapeDtypeStruct((CONV_W - 1, NB, W_B), F32),
                 jax.ShapeDtypeStruct((NB, W_B), F32)]
    args = [x, *layer_params]
    if cast_mlp:
        c_in, c_out, c_shape = _cast_specs(*cast_mlp, n_steps)
        in_specs += c_in
        out_specs += c_out
        out_shape += c_shape
        args += list(cast_mlp[:2])
    return pl.pallas_call(
        functools.partial(_even_prompt_kernel, batch_major_in=batch_major_in,
                          cast_mlp=bool(cast_mlp)),
        grid=(n_steps,),
        in_specs=in_specs,
        out_specs=out_specs,
        out_shape=out_shape,
        scratch_shapes=scratch,
        compiler_params=_params(1),
        name="even_prompt",
    )(*args)


def _even_sample_kernel(x_ref, conv_ref, h0_ref, g_ref, win_ref, wout_ref, vn_ref,
                        w00_ref, b0_ref, cw_ref, cb_ref, wg_ref, bg_ref, lam_ref,
                        xo_ref, v_ref, convo_ref, ho_ref):
    x = x_ref[...]
    xn = _rms(x, g_ref[...]).astype(BF16)
    proj = _dot(xn, win_ref[...])
    u = _gelu(proj[:, :W_A])
    v = _rms(_gelu(proj[:, W_A:2 * W_A]), vn_ref[...])
    gate = proj[:, 2 * W_A:2 * W_A + W_B]
    xb = proj[:, 2 * W_A + W_B:]
    v_ref[...] = v
    a_out = (u * (v * w00_ref[...] + b0_ref[...])).astype(BF16)

    cw = cw_ref[...]
    xc = cb_ref[...] + xb * cw[CONV_W - 1:CONV_W, :]
    for k in range(CONV_W - 1):
        hk = conv_ref[k]
        xc = xc + hk * cw[k:k + 1, :]
        if k > 0:
            convo_ref[k - 1] = hk
    convo_ref[CONV_W - 2] = xb

    a, bt = _lru_coeffs(xc, wg_ref, bg_ref, lam_ref)
    h = a * h0_ref[...] + bt
    ho_ref[...] = h
    b_out = (h * _gelu(gate)).astype(BF16)
    y = _dot(jnp.concatenate([a_out, b_out], axis=1), wout_ref[...])
    xo_ref[...] = x + y


def _even_sample(x, state_conv, state_h, layer_params, e):
    n = x.shape[0]
    return pl.pallas_call(
        _even_sample_kernel,
        grid=(1,),
        in_specs=[_const_spec(x.shape), _layer_spec(state_conv, e), _layer_spec(state_h, e)]
                 + [_layer_spec(p, e) for p in layer_params],
        out_specs=[_const_spec((n, D_MODEL)), _const_spec((n, W_A)),
                   _const_spec(state_conv.shape[1:]), _const_spec((n, W_B))],
        out_shape=[jax.ShapeDtypeStruct((n, D_MODEL), F32),
                   jax.ShapeDtypeStruct((n, W_A), F32),
                   jax.ShapeDtypeStruct(state_conv.shape[1:], F32),
                   jax.ShapeDtypeStruct((n, W_B), F32)],
        compiler_params=_params(1),
        name="even_sample",
    )(x, state_conv, state_h, *layer_params)


def _drive(*stage_gens):
    results = [None] * len(stage_gens)
    live = list(range(len(stage_gens)))
    while live:
        for i in list(live):
            try:
                next(stage_gens[i])
            except StopIteration as stop:
                results[i] = stop.value
                live.remove(i)
    return results


def _pool_mix(x, t0, z_buf, g_ref, wp_ref, bp_ref, sc_ref, emit):
    rows = x.shape[0]
    pad = (POOL_HIST + 1) * NB
    xn = _rms(x, g_ref[...])
    z_buf[pad:pad + rows, :] = xn
    row = lax.broadcasted_iota(jnp.int32, (rows, 1), 0)
    pos = t0 + jnp.right_shift(row, NB.bit_length() - 1)
    ys = []
    for gi, w in enumerate(POOL_WINDOWS):
        yield
        c0 = gi * G_POOL
        acc = z_buf[pad - (w - 1) * NB:pad + rows, c0:c0 + G_POOL]
        span = 1
        while span < w:
            acc = acc[span * NB:, :] + acc[:-span * NB, :]
            span *= 2
        cnt = jnp.minimum(pos + 1, w).astype(F32)
        p = (acc / cnt - xn[:, c0:c0 + G_POOL]).astype(BF16)
        ys.append(_dot(p, wp_ref[gi]))
    y = (jnp.concatenate(ys, axis=1) + bp_ref[...]) * sc_ref[...]
    tail = z_buf[rows + NB:rows + pad, :]
    z_buf[NB:pad, :] = tail
    emit(x + y, tail)


def _pool_mlp_kernel(x_ref, gm_ref, wp_ref, bp_ref, sc_ref, gf_ref, w1_ref, w2_ref, gfin_ref, *refs,
                     final_norm, with_sample):
    if with_sample:
        xs_ref, o_ref, hist_ref, xso_ref, z_buf, p_buf, *maybe_ot_buf = refs
    else:
        o_ref, hist_ref, z_buf, p_buf, *maybe_ot_buf = refs
    rows = x_ref.shape[0]
    pad = (POOL_HIST + 1) * NB
    j = pl.program_id(0)
    n_tiles = pl.num_programs(0) - 1

    def pool_stages():
        return _pool_mix(x_ref[...], j * (rows // NB), z_buf, gm_ref, wp_ref, bp_ref, sc_ref,
                         finish_pool)

    def mlp_stages():
        return _mlp_rows(p_buf[lax.rem(j + 1, 2)], gf_ref, w1_ref, w2_ref)

    def finish_pool(xm, tail):
        slot = lax.rem(j, 2)
        if final_norm:
            _store_batch_major(xm, p_buf.at[slot], *maybe_ot_buf)
        else:
            p_buf[slot] = xm
        hist_ref[...] = tail.reshape(POOL_HIST, NB, D_MODEL)

    def finish_mlp(acc):
        if final_norm:
            o_ref[...] = _rms(acc, gfin_ref[...]).reshape(o_ref.shape)
        else:
            o_ref[...] = acc

    @pl.when(j == 0)
    def _():
        z_buf[0:pad, :] = jnp.zeros((pad, D_MODEL), F32)
        _drive(pool_stages())

    @pl.when(jnp.logical_and(j > 0, j < n_tiles))
    def _():
        acc, _ = _drive(mlp_stages(), pool_stages())
        finish_mlp(acc)

    @pl.when(j == n_tiles)
    def _():
        finish_mlp(*_drive(mlp_stages()))
        if with_sample:
            acc_s, = _drive(_mlp_rows(xs_ref[...], gf_ref, w1_ref, w2_ref))
            xso_ref[...] = _rms(acc_s, gfin_ref[...]) if final_norm else acc_s


def _pool_mlp(x, xs, pool_params, o, g_ffn, w1, w2, g_fin, layer, *, final_norm):
    with_sample = xs is not None
    xs_args = [xs] if with_sample else []
    xs_specs = [_const_spec(xs.shape)] if with_sample else []
    xs_shapes = [jax.ShapeDtypeStruct(xs.shape, F32)] if with_sample else []
    n_rows = x.shape[0]
    rows = POOL_MLP_T * NB
    n_tiles = n_rows // rows
    x_spec = pl.BlockSpec((rows, D_MODEL), lambda j: (jnp.minimum(j, n_tiles - 1), 0))
    if final_norm:
        out_spec = pl.BlockSpec((NB, POOL_MLP_T, D_MODEL), lambda j: (0, jnp.maximum(j - 1, 0), 0))
        out_shape = jax.ShapeDtypeStruct((NB, n_rows // NB, D_MODEL), F32)
        extra_scratch = [pltpu.VMEM((D_MODEL // LANES, rows, LANES), F32)]
    else:
        out_spec = pl.BlockSpec((rows, D_MODEL), lambda j: (jnp.maximum(j - 1, 0), 0))
        out_shape = jax.ShapeDtypeStruct(x.shape, F32)
        extra_scratch = []
    return pl.pallas_call(
        functools.partial(_pool_mlp_kernel, final_norm=final_norm, with_sample=with_sample),
        grid=(n_tiles + 1,),
        in_specs=[x_spec] + [_layer_spec(p, o) for p in pool_params]
                 + [_layer_spec(g_ffn, layer), _const_spec(w1.shape), _const_spec(w2.shape),
                    _const_spec(g_fin.shape)] + xs_specs,
        out_specs=[out_spec, _const_spec((POOL_HIST, NB, D_MODEL))] + xs_specs,
        out_shape=[out_shape, jax.ShapeDtypeStruct((POOL_HIST, NB, D_MODEL), F32)] + xs_shapes,
        scratch_shapes=[pltpu.VMEM(((POOL_HIST + 1) * NB + rows, D_MODEL), F32),
                        pltpu.VMEM((2, rows, D_MODEL), F32)] + extra_scratch,
        compiler_params=_params(1),
        name="pool_mlp",
    )(x, *pool_params, g_ffn, w1, w2, g_fin, *xs_args)


def _mlp_sample_kernel(x_ref, g_ref, w1_ref, w2_ref, gfin_ref, o_ref):
    acc, = _drive(_mlp_rows(x_ref[...], g_ref, w1_ref, w2_ref))
    o_ref[...] = _rms(acc, gfin_ref[...])


def _mlp_sample_final(xs, g, w1, w2, g_fin, layer):
    return pl.pallas_call(
        _mlp_sample_kernel,
        grid=(1,),
        in_specs=[_const_spec(xs.shape), _layer_spec(g, layer), _const_spec(w1.shape),
                  _const_spec(w2.shape), _const_spec(g_fin.shape)],
        out_specs=_const_spec(xs.shape),
        out_shape=jax.ShapeDtypeStruct(xs.shape, F32),
        compiler_params=_params(1),
        name="mlp_sample",
    )(xs, g, w1, w2, g_fin)


def _pool_sample_kernel(x_ref, hist_ref, g_ref, wp_ref, bp_ref, sc_ref, xo_ref, histo_ref):
    x = x_ref[...]
    xn = _rms(x, g_ref[...])
    ys = []
    for gi, w in enumerate(POOL_WINDOWS):
        c0 = gi * G_POOL
        xg = xn[:, c0:c0 + G_POOL]
        wsum = xg
        for d in range(1, w):
            wsum = wsum + hist_ref[POOL_HIST - d, :, c0:c0 + G_POOL]
        cnt = float(min(PAST_LEN + 1, w))
        p = (wsum / cnt - xg).astype(BF16)
        ys.append(_dot(p, wp_ref[gi]))
    y = (jnp.concatenate(ys, axis=1) + bp_ref[...]) * sc_ref[...]
    xo_ref[...] = x + y
    histo_ref[0:POOL_HIST - 1] = hist_ref[1:POOL_HIST]
    histo_ref[POOL_HIST - 1] = xn


def _pool_sample(x, state_pool, layer_params, o):
    n = x.shape[0]
    return pl.pallas_call(
        _pool_sample_kernel,
        grid=(1,),
        in_specs=[_const_spec(x.shape), _layer_spec(state_pool, o)]
                 + [_layer_spec(p, o) for p in layer_params],
        out_specs=[_const_spec((n, D_MODEL)), _const_spec(state_pool.shape[1:])],
        out_shape=[jax.ShapeDtypeStruct((n, D_MODEL), F32),
                   jax.ShapeDtypeStruct(state_pool.shape[1:], F32)],
        compiler_params=_params(1),
        name="pool_sample",
    )(x, state_pool, *layer_params)


def _cast_specs(w1_f32, w2_f32, layer, n_steps):
    r1, r2 = D_MODEL // n_steps, D_FF // n_steps
    in_specs = [pl.BlockSpec((None, r1, D_FF), lambda i: (layer, i, 0)),
                pl.BlockSpec((None, r2, D_MODEL), lambda i: (layer, i, 0))]
    out_specs = [pl.BlockSpec((r1, D_FF), lambda i: (i, 0)),
                 pl.BlockSpec((r2, D_MODEL), lambda i: (i, 0))]
    out_shape = [jax.ShapeDtypeStruct((D_MODEL, D_FF), BF16),
                 jax.ShapeDtypeStruct((D_FF, D_MODEL), BF16)]
    return in_specs, out_specs, out_shape


def _mlp_rows(x, g_ref, w1_ref, w2_ref):
    xn = _rms(x, g_ref[...]).astype(BF16)
    acc = x
    for c in range(D_FF // FF_CHUNK):
        yield
        h = _dot(xn, w1_ref[:, c * FF_CHUNK:(c + 1) * FF_CHUNK])
        h = jnp.square(jnp.maximum(h, 0.0)).astype(BF16)
        acc = acc + _dot(h, w2_ref[c * FF_CHUNK:(c + 1) * FF_CHUNK, :])
    return acc


def _store_batch_major(rows_tm, dst_ref, ot_buf):
    t_steps = rows_tm.shape[0] // NB
    for c in range(D_MODEL // LANES):
        ot_buf[c] = rows_tm[:, c * LANES:(c + 1) * LANES]
    for b in range(NB):
        for c in range(D_MODEL // LANES):
            dst_ref[b * t_steps:(b + 1) * t_steps, c * LANES:(c + 1) * LANES] = (
                ot_buf[c, pl.ds(b, t_steps, stride=NB), :])


def _mlp_kernel(x_ref, xs_ref, g_ref, w1_ref, w2_ref, w1n_ref, w2n_ref,
                o_ref, xso_ref, w1o_ref, w2o_ref):
    w1o_ref[...] = w1n_ref[...].astype(BF16)
    w2o_ref[...] = w2n_ref[...].astype(BF16)
    acc, = _drive(_mlp_rows(x_ref[...], g_ref, w1_ref, w2_ref))
    o_ref[...] = acc

    @pl.when(pl.program_id(0) == 0)
    def _():
        acc_s, = _drive(_mlp_rows(xs_ref[...], g_ref, w1_ref, w2_ref))
        xso_ref[...] = acc_s


def _mlp(x, xs, g, w1, w2, next_f32, layer, *, tm):
    n_steps = x.shape[0] // tm
    x_spec = pl.BlockSpec((tm, D_MODEL), lambda i: (i, 0))
    c_in, c_out, c_shape = _cast_specs(*next_f32, layer + 1, n_steps)
    return pl.pallas_call(
        _mlp_kernel,
        grid=(n_steps,),
        in_specs=[x_spec, _const_spec(xs.shape), _layer_spec(g, layer), _const_spec(w1.shape),
                  _const_spec(w2.shape)] + c_in,
        out_specs=[x_spec, _const_spec(xs.shape)] + c_out,
        out_shape=[jax.ShapeDtypeStruct(x.shape, F32), jax.ShapeDtypeStruct(xs.shape, F32)] + c_shape,
        compiler_params=_params(1),
        name="mlp",
    )(x, xs, g, w1, w2, *next_f32)


def _gate_blocks(wa, wx):
    n_e = wa.shape[0]
    hpb = MXU_K // DH_B
    eye = jnp.eye(hpb, dtype=wa.dtype)

    def bd(w):
        w = w.reshape(n_e, H_B // hpb, hpb, DH_B, DH_B)
        return jnp.einsum('eghij,hk->eghikj', w, eye).reshape(n_e, H_B // hpb, MXU_K, MXU_K)

    return jnp.concatenate([bd(wa), bd(wx)], axis=-1)


def _rows(v):
    return v.reshape(v.shape[0], 1, v.shape[1])


def kernel(x_prompt, x_sample, state_conv, state_rglru, state_pool, norm_mix, norm_ffn, norm_final, w_in, w_out, v_norm, sgu_w, sgu_b, conv_w, conv_b, gate_a_w, gate_a_b, gate_x_w, gate_x_b, lru_lambda, pool_w, pool_b, pool_scale, ffn_w1, ffn_w2):
    n_p, seq, _ = x_prompt.shape
    n_s = x_sample.shape[0]
    depth = norm_mix.shape[0]
    assert n_p == NB and seq % CHUNK == 0

    tril = jnp.tril(jnp.ones((CHUNK, CHUNK), dtype=bool))
    wm = jnp.where(tril, sgu_w, 0.0)
    sw = jnp.concatenate([wm[:, 0::2], wm[:, 1::2]], axis=3).astype(BF16)
    sb = jnp.repeat(jnp.swapaxes(sgu_b, 1, 2), DH_A, axis=2)
    wg = _gate_blocks(gate_a_w, gate_x_w).astype(BF16)
    bg = _rows(jnp.concatenate([gate_a_b, gate_x_b], axis=1))
    w00 = _rows(jnp.repeat(wm[:, :, 0, 0], DH_A, axis=1))
    b0 = _rows(jnp.repeat(sgu_b[:, :, 0], DH_A, axis=1))
    win, wout = w_in.astype(BF16), w_out.astype(BF16)
    wp = pool_w.astype(BF16)
    g_mix_even, g_mix_odd = _rows(norm_mix[0::2]), _rows(norm_mix[1::2])
    g_ffn = _rows(norm_ffn)
    g_fin = norm_final.reshape(1, D_MODEL)
    even_tail = (conv_w, _rows(conv_b), wg, bg, _rows(lru_lambda))
    even_p = (g_mix_even, win, wout, _rows(v_norm), sw, sb) + even_tail
    even_s = (g_mix_even, win, wout, _rows(v_norm), w00, b0) + even_tail
    pool_params = (g_mix_odd, wp, _rows(pool_b), _rows(pool_scale))

    conv_state = jnp.swapaxes(state_conv, 1, 2)
    pool_state = jnp.swapaxes(state_pool, 1, 2)

    xp = x_prompt
    xs = x_sample.reshape(n_s, D_MODEL)
    sgu_v_s, conv_p, conv_s, h_p, h_s, pool_p, pool_s = [], [], [], [], [], [], []
    assert depth % 2 == 0
    for layer in range(depth):
        last = layer == depth - 1
        if layer % 2 == 0:
            e = layer // 2
            xp, cp, hp, w1, w2 = _even_prompt(xp, even_p, e, batch_major_in=(layer == 0),
                                              cast_mlp=(ffn_w1, ffn_w2, layer))
            xs, vs, cs, hs = _even_sample(xs, conv_state, state_rglru, even_s, e)
            sgu_v_s.append(vs.reshape(n_s, 1, W_A))
            conv_p.append(cp)
            conv_s.append(cs)
            h_p.append(hp)
            h_s.append(hs)
            xp, xs, w1, w2 = _mlp(xp, xs, g_ffn, w1, w2, (ffn_w1, ffn_w2), layer, tm=MLP_TM)
        else:
            o = layer // 2
            xs, ps = _pool_sample(xs, pool_state, pool_params, o)
            if last:
                xs = _mlp_sample_final(xs, g_ffn, w1, w2, g_fin, layer)
                xp, pp = _pool_mlp(xp, None, pool_params, o, g_ffn, w1, w2, g_fin, layer,
                                   final_norm=True)
            else:
                xp, pp, xs = _pool_mlp(xp, xs, pool_params, o, g_ffn, w1, w2, g_fin, layer,
                                       final_norm=False)
            pool_p.append(pp)
            pool_s.append(ps)

    def stack_positions(parts):
        return jnp.swapaxes(jnp.stack(parts), 1, 2)

    return (xp, xs.reshape(n_s, 1, D_MODEL), jnp.stack(sgu_v_s), stack_positions(conv_p),
            stack_positions(conv_s), jnp.stack(h_p), jnp.stack(h_s), stack_positions(pool_p),
            stack_positions(pool_s))
```
